```python
import math
import jax, jax.numpy as jnp
from jax import lax
import numpy as np

D_MODEL = 4096
BATCH = 8
SEQ = 2048
DEPTH = 4

N_HEADS = 8
HEAD_DIM = 128
ATTN_W = N_HEADS * 2 * HEAD_DIM
Q_BLOCK = 128
LRU_W = 1024
LRU_BLOCKS = 8
LRU_BLOCK_W = LRU_W // LRU_BLOCKS
CONV_W = 4
LRU_C = 8.0
D_FF = 4 * D_MODEL
EPS = 1e-6
C_IN = 3 * ATTN_W + 2 * LRU_W + 2 * D_MODEL

kernel_name = "hybrid_diffattn_rglru_gated_block"


def rmsnorm(x, g):
    xf = x.astype(jnp.float32)
    y = xf * lax.rsqrt(jnp.mean(xf * xf, axis=-1, keepdims=True) + EPS)
    return (y * g.astype(jnp.float32)).astype(x.dtype)


def diff_attention(q, k, v, lam):
    B, S = q.shape[0], q.shape[1]
    nqb = S // Q_BLOCK
    qf = q.astype(jnp.float32) * (HEAD_DIM ** -0.5)
    kf = k.astype(jnp.float32)
    vf = v.astype(jnp.float32)
    qb = qf.reshape(B, nqb, Q_BLOCK, N_HEADS, 2, HEAD_DIM).transpose(1, 0, 2, 3, 4, 5)
    k_pos = jnp.arange(S)

    def one_block(args):
        q_blk, i = args
        s = jnp.einsum('bqhcd,bkhcd->bhcqk', q_blk, kf)
        q_pos = i * Q_BLOCK + jnp.arange(Q_BLOCK)
        mask = q_pos[:, None] >= k_pos[None, :]
        p = jax.nn.softmax(jnp.where(mask, s, -jnp.inf), axis=-1)
        w = p[:, :, 0] - lam * p[:, :, 1]
        return jnp.einsum('bhqk,bkhe->bqhe', w, vf)

    out = lax.map(one_block, (qb, jnp.arange(nqb)))
    return out.transpose(1, 0, 2, 3, 4).reshape(B, S, N_HEADS, 2 * HEAD_DIM)


def causal_conv(x, w, b):
    y = lax.conv_general_dilated(
        x, w[:, None, :].astype(x.dtype), window_strides=(1,), padding=[(CONV_W - 1, 0)],
        dimension_numbers=('NWC', 'WIO', 'NWC'), feature_group_count=x.shape[-1])
    return y + b.astype(x.dtype)


def rg_lru(x, w_r, b_r, w_i, b_i, lam):
    B, S, R = x.shape
    xf = x.astype(jnp.float32)
    xb = xf.reshape(B, S, LRU_BLOCKS, LRU_BLOCK_W)
    r = jax.nn.sigmoid(jnp.einsum('bsnc,ncd->bsnd', xb, w_r.astype(jnp.float32)).reshape(B, S, R)
                       + b_r.astype(jnp.float32))
    i = jax.nn.sigmoid(jnp.einsum('bsnc,ncd->bsnd', xb, w_i.astype(jnp.float32)).reshape(B, S, R)
                       + b_i.astype(jnp.float32))
    log_a = -LRU_C * r * jax.nn.softplus(-lam.astype(jnp.float32))
    a = jnp.exp(log_a)
    bt = jnp.sqrt(-jnp.expm1(2.0 * log_a)) * (i * xf)

    def combine(left, right):
        a1, b1 = left
        a2, b2 = right
        return a1 * a2, a2 * b1 + b2

    _, h = lax.associative_scan(combine, (a, bt), axis=1)
    return h


def setup_inputs(seed: int = 0) -> dict:
    key = jax.random.key(seed)
    ks = jax.random.split(key, 24)
    f32 = jnp.float32
    nrm = lambda k, shape, scale: jax.random.normal(k, shape, f32) * scale
    u = jax.random.uniform(ks[13], (DEPTH, LRU_W), f32, 0.9, 0.999)
    p = u ** (1.0 / LRU_C)
    return {
        "x": jax.random.normal(ks[0], (BATCH, SEQ, D_MODEL), f32),
        "norm1_g": 1.0 + nrm(ks[1], (DEPTH, D_MODEL), 0.01),
        "w_in": nrm(ks[2], (DEPTH, D_MODEL, C_IN), D_MODEL ** -0.5),
        "gate_b": nrm(ks[3], (DEPTH, 2, D_MODEL), 0.01),
        "lam_qk": nrm(ks[4], (DEPTH, 4, HEAD_DIM), 0.1),
        "subln_g": 1.0 + nrm(ks[5], (DEPTH, 2 * HEAD_DIM), 0.01),
        "w_attn_proj": nrm(ks[6], (DEPTH, ATTN_W, D_MODEL), ATTN_W ** -0.5),
        "conv_w": nrm(ks[7], (DEPTH, CONV_W, LRU_W), CONV_W ** -0.5),
        "conv_b": nrm(ks[8], (DEPTH, LRU_W), 0.01),
        "w_rgate": nrm(ks[9], (DEPTH, LRU_BLOCKS, LRU_BLOCK_W, LRU_BLOCK_W), LRU_BLOCK_W ** -0.5),
        "b_rgate": nrm(ks[10], (DEPTH, LRU_W), 0.01),
        "w_igate": nrm(ks[11], (DEPTH, LRU_BLOCKS, LRU_BLOCK_W, LRU_BLOCK_W), LRU_BLOCK_W ** -0.5),
        "b_igate": nrm(ks[12], (DEPTH, LRU_W), 0.01),
        "lru_lambda": jnp.log(p) - jnp.log1p(-p),
        "w_rec_proj": nrm(ks[14], (DEPTH, LRU_W, D_MODEL), LRU_W ** -0.5),
        "w_out": nrm(ks[15], (DEPTH, D_MODEL, D_MODEL), D_MODEL ** -0.5),
        "norm2_g": 1.0 + nrm(ks[16], (DEPTH, D_MODEL), 0.01),
        "w_up": nrm(ks[17], (DEPTH, D_MODEL, D_FF), D_MODEL ** -0.5),
        "w_down": nrm(ks[18], (DEPTH, D_FF, D_MODEL), D_FF ** -0.5),
        "final_g": 1.0 + nrm(ks[19], (D_MODEL,), 0.01),
    }


def reference(x, norm1_g, w_in, gate_b, lam_qk, subln_g, w_attn_proj, conv_w, conv_b,
              w_rgate, b_rgate, w_igate, b_igate, lru_lambda, w_rec_proj, w_out,
              norm2_g, w_up, w_down, final_g):
    B, S, _ = x.shape
    splits = np.cumsum([ATTN_W, ATTN_W, ATTN_W, LRU_W, LRU_W, D_MODEL]).tolist()
    for l in range(DEPTH):
        h = rmsnorm(x, norm1_g[l])
        z = h @ w_in[l]
        zq, zk, zv, zlx, zly, zga, zgr = jnp.split(z, splits, axis=-1)
        lam_init = 0.8 - 0.6 * math.exp(-0.3 * l)
        lq = lam_qk[l].astype(jnp.float32)
        lam = jnp.exp(jnp.sum(lq[0] * lq[1])) - jnp.exp(jnp.sum(lq[2] * lq[3])) + lam_init
        q = zq.reshape(B, S, N_HEADS, 2, HEAD_DIM)
        k = zk.reshape(B, S, N_HEADS, 2, HEAD_DIM)
        v = zv.reshape(B, S, N_HEADS, 2 * HEAD_DIM)
        att = diff_attention(q, k, v, lam)
        att = rmsnorm(att, subln_g[l]) * (1.0 - lam_init)
        y_att = att.reshape(B, S, ATTN_W).astype(x.dtype) @ w_attn_proj[l]
        xr = causal_conv(zlx, conv_w[l], conv_b[l])
        hr = rg_lru(xr, w_rgate[l], b_rgate[l], w_igate[l], b_igate[l], lru_lambda[l])
        rec = (hr * jax.nn.gelu(zly.astype(jnp.float32))).astype(x.dtype)
        y_rec = rec @ w_rec_proj[l]
        g_att = jax.nn.sigmoid(zga + gate_b[l, 0])
        g_rec = jax.nn.sigmoid(zgr + gate_b[l, 1])
        x = x + (g_att * y_att + g_rec * y_rec) @ w_out[l]
        h2 = rmsnorm(x, norm2_g[l])
        x = x + jnp.square(jax.nn.relu(h2 @ w_up[l])) @ w_down[l]
    return rmsnorm(x, final_g)
```

```python
import functools
import math

import numpy as np
import jax
import jax.numpy as jnp
from jax import lax
from jax.experimental import pallas as pl
from jax.experimental.pallas import tpu as pltpu

D_MODEL = 4096
DEPTH = 4
N_HEADS = 8
HEAD_DIM = 128
HEAD_W = 2 * HEAD_DIM
ATTN_W = N_HEADS * HEAD_W
LRU_W = 1024
LRU_BLOCKS = 8
LRU_BLOCK_W = LRU_W // LRU_BLOCKS
CONV_W = 4
LRU_C = 8.0
D_FF = 4 * D_MODEL
EPS = 1e-6
C_IN = 3 * ATTN_W + 2 * LRU_W + 2 * D_MODEL
OFF_Q, OFF_K, OFF_V = 0, ATTN_W, 2 * ATTN_W
OFF_LX = 3 * ATTN_W
OFF_LY = OFF_LX + LRU_W
OFF_GA = OFF_LY + LRU_W
OFF_GR = OFF_GA + D_MODEL

LANES = 128
SUBLANES = 8
BF16_ROWS = 16
VMEM_LIMIT = 56 * 1024 * 1024

BF16 = jnp.bfloat16
F32 = jnp.float32


def _params(*sem):
    return pltpu.CompilerParams(dimension_semantics=sem, vmem_limit_bytes=VMEM_LIMIT)


def _rmsnorm_kernel(x_ref, g_ref, o_ref):
    x = x_ref[...]
    y = x * lax.rsqrt(jnp.mean(x * x, axis=-1, keepdims=True) + EPS)
    o_ref[...] = (y * g_ref[...]).astype(o_ref.dtype)


def _rmsnorm(x, g, out_dtype, bm=256):
    t, d = x.shape
    return pl.pallas_call(
        _rmsnorm_kernel,
        grid=(t // bm,),
        in_specs=[pl.BlockSpec((bm, d), lambda i: (i, 0)),
                  pl.BlockSpec((1, d), lambda i: (0, 0))],
        out_specs=pl.BlockSpec((bm, d), lambda i: (i, 0)),
        out_shape=jax.ShapeDtypeStruct((t, d), out_dtype),
        compiler_params=_params("parallel"),
        name="rmsnorm",
    )(x, g.reshape(1, d))


def _in_proj_kernel(a_ref, b_ref, cs_ref, o_ref):
    acc = jnp.dot(a_ref[...], b_ref[...], preferred_element_type=F32)
    o_ref[...] = (acc * cs_ref[...]).astype(o_ref.dtype)


def _in_proj(h, w, colscale, bm=1024, bn=1024):
    t, k = h.shape
    n = w.shape[1]
    return pl.pallas_call(
        _in_proj_kernel,
        grid=(t // bm, n // bn),
        in_specs=[pl.BlockSpec((bm, k), lambda i, j: (i, 0)),
                  pl.BlockSpec((k, bn), lambda i, j: (0, j)),
                  pl.BlockSpec((1, bn), lambda i, j: (0, j))],
        out_specs=pl.BlockSpec((bm, bn), lambda i, j: (i, j)),
        out_shape=jax.ShapeDtypeStruct((t, n), BF16),
        compiler_params=_params("parallel", "parallel"),
        name="in_proj",
    )(h, w, colscale)


def _up_proj_kernel(a_ref, b_ref, o_ref):
    acc = jnp.dot(a_ref[...], b_ref[...], preferred_element_type=F32)
    o_ref[...] = jnp.square(jnp.maximum(acc, 0.0)).astype(o_ref.dtype)


def _up_proj(h, w, bm=1024, bn=1024):
    t, k = h.shape
    n = w.shape[1]
    return pl.pallas_call(
        _up_proj_kernel,
        grid=(t // bm, n // bn),
        in_specs=[pl.BlockSpec((bm, k), lambda i, j: (i, 0)),
                  pl.BlockSpec((k, bn), lambda i, j: (0, j))],
        out_specs=pl.BlockSpec((bm, bn), lambda i, j: (i, j)),
        out_shape=jax.ShapeDtypeStruct((t, n), BF16),
        compiler_params=_params("parallel", "parallel"),
        name="up_proj",
    )(h, w)


def _merge_kernel(att_ref, rec_ref, wa_ref, wr_ref, ga_ref, gr_ref, gb_ref, o_ref):
    y_att = jnp.dot(att_ref[...], wa_ref[...], preferred_element_type=F32)
    y_rec = jnp.dot(rec_ref[...], wr_ref[...], preferred_element_type=F32)
    g_att = jax.nn.sigmoid(ga_ref[...].astype(F32) + gb_ref[0:1, :])
    g_rec = jax.nn.sigmoid(gr_ref[...].astype(F32) + gb_ref[1:2, :])
    o_ref[...] = (g_att * y_att + g_rec * y_rec).astype(o_ref.dtype)


def _merge(att, rec, w_att, w_rec, z, gate_b, bm=1024, bn=1024):
    t = att.shape[0]
    n = w_att.shape[1]
    ga_blk, gr_blk = OFF_GA // bn, OFF_GR // bn
    return pl.pallas_call(
        _merge_kernel,
        grid=(t // bm, n // bn),
        in_specs=[pl.BlockSpec((bm, ATTN_W), lambda i, j: (i, 0)),
                  pl.BlockSpec((bm, LRU_W), lambda i, j: (i, 0)),
                  pl.BlockSpec((ATTN_W, bn), lambda i, j: (0, j)),
                  pl.BlockSpec((LRU_W, bn), lambda i, j: (0, j)),
                  pl.BlockSpec((bm, bn), lambda i, j: (i, j + ga_blk)),
                  pl.BlockSpec((bm, bn), lambda i, j: (i, j + gr_blk)),
                  pl.BlockSpec((2, bn), lambda i, j: (0, j))],
        out_specs=pl.BlockSpec((bm, bn), lambda i, j: (i, j)),
        out_shape=jax.ShapeDtypeStruct((t, n), BF16),
        compiler_params=_params("parallel", "parallel"),
        name="merge",
    )(att, rec, w_att, w_rec, z, z, gate_b)


def _residual_matmul_kernel(a_ref, b_ref, x_ref, o_ref, acc_ref, *, nk):
    k = pl.program_id(2)

    @pl.when(k == 0)
    def _():
        acc_ref[...] = jnp.zeros_like(acc_ref)

    acc_ref[...] += jnp.dot(a_ref[...], b_ref[...], preferred_element_type=F32)

    @pl.when(k == nk - 1)
    def _():
        o_ref[...] = x_ref[...] + acc_ref[...]


def _residual_matmul(a, w, x, bm, bn, bk, name):
    t, kdim = a.shape
    n = w.shape[1]
    nk = kdim // bk
    return pl.pallas_call(
        functools.partial(_residual_matmul_kernel, nk=nk),
        grid=(t // bm, n // bn, nk),
        in_specs=[pl.BlockSpec((bm, bk), lambda i, j, k: (i, k)),
                  pl.BlockSpec((bk, bn), lambda i, j, k: (k, j)),
                  pl.BlockSpec((bm, bn), lambda i, j, k: (i, j))],
        out_specs=pl.BlockSpec((bm, bn), lambda i, j, k: (i, j)),
        out_shape=jax.ShapeDtypeStruct((t, n), F32),
        scratch_shapes=[pltpu.VMEM((bm, bn), F32)],
        compiler_params=_params("parallel", "parallel", "arbitrary"),
        name=name,
    )(a, w, x)


def _attn_kernel(q_ref, k_ref, v_ref, lq_ref, sg_ref, o_ref, *, lam_init, tq):
    seq = q_ref.shape[1]
    lq = lq_ref[...]
    lam = (jnp.exp(jnp.sum(lq[0:1] * lq[1:2], axis=-1, keepdims=True))
           - jnp.exp(jnp.sum(lq[2:3] * lq[3:4], axis=-1, keepdims=True)) + lam_init)
    contract_last = (((1,), (1,)), ((), ()))
    for i in range(seq // tq):
        kv_len = (i + 1) * tq
        q = q_ref[0, i * tq:(i + 1) * tq, :]
        k = k_ref[0, 0:kv_len, :]
        v = v_ref[0, 0:kv_len, :]
        row = lax.broadcasted_iota(jnp.int32, (tq, kv_len), 0) + i * tq
        col = lax.broadcasted_iota(jnp.int32, (tq, kv_len), 1)
        visible = row >= col
        probs = []
        for c in range(2):
            s = lax.dot_general(q[:, c * HEAD_DIM:(c + 1) * HEAD_DIM], k[:, c * HEAD_DIM:(c + 1) * HEAD_DIM],
                                contract_last, preferred_element_type=F32)
            s = jnp.where(visible, s, -jnp.inf)
            e = jnp.exp(s - jnp.max(s, axis=-1, keepdims=True))
            probs.append(e * (1.0 / jnp.sum(e, axis=-1, keepdims=True)))
        w = probs[0] - lam * probs[1]
        o = jnp.dot(w.astype(BF16), v, preferred_element_type=F32)
        o = o * lax.rsqrt(jnp.mean(o * o, axis=-1, keepdims=True) + EPS)
        o = (o * sg_ref[...]) * (1.0 - lam_init)
        o_ref[0, i * tq:(i + 1) * tq, :] = o.astype(o_ref.dtype)


def _attention(z3, lam_qk, subln_g, lam_init, tq=256):
    b, s, _ = z3.shape
    kb, vb = OFF_K // HEAD_W, OFF_V // HEAD_W
    return pl.pallas_call(
        functools.partial(_attn_kernel, lam_init=lam_init, tq=tq),
        grid=(b, N_HEADS),
        in_specs=[pl.BlockSpec((1, s, HEAD_W), lambda bi, h: (bi, 0, h)),
                  pl.BlockSpec((1, s, HEAD_W), lambda bi, h: (bi, 0, h + kb)),
                  pl.BlockSpec((1, s, HEAD_W), lambda bi, h: (bi, 0, h + vb)),
                  pl.BlockSpec((4, HEAD_DIM), lambda bi, h: (0, 0)),
                  pl.BlockSpec((1, HEAD_W), lambda bi, h: (0, 0))],
        out_specs=pl.BlockSpec((1, s, HEAD_W), lambda bi, h: (bi, 0, h)),
        out_shape=jax.ShapeDtypeStruct((b, s, ATTN_W), BF16),
        compiler_params=_params("parallel", "parallel"),
        name="diff_attention",
    )(z3, z3, z3, lam_qk, subln_g.reshape(1, HEAD_W))


def _softplus(x):
    return jnp.maximum(x, 0.0) + jnp.log1p(jnp.exp(-jnp.abs(x)))


def _gelu_tanh(x):
    sqrt_2_over_pi = np.sqrt(2 / np.pi).astype(np.float32)
    cdf = 0.5 * (1.0 + jnp.tanh(sqrt_2_over_pi * (x + 0.044715 * (x * x * x))))
    return x * cdf


def _rglru_kernel(lx_ref, lxp_ref, ly_ref, cw_ref, cb_ref, wg_ref, br_ref, bi_ref, lam_ref,
                  o_ref, carry_ref, *, ts):
    t = pl.program_id(1)

    @pl.when(t == 0)
    def _():
        carry_ref[...] = jnp.zeros_like(carry_ref)

    row = lax.broadcasted_iota(jnp.int32, (ts, LRU_BLOCK_W), 0)
    row8 = lax.broadcasted_iota(jnp.int32, (SUBLANES, LRU_BLOCK_W), 0)
    for n in range(LRU_BLOCKS):
        sl = slice(n * LRU_BLOCK_W, (n + 1) * LRU_BLOCK_W)
        x = lx_ref[0, :, sl].astype(F32)
        tail = lxp_ref[0, :, sl].astype(F32)[BF16_ROWS - SUBLANES:, :]
        tail = jnp.where(t > 0, tail, 0.0)
        xr = cb_ref[:, sl] + cw_ref[CONV_W - 1:CONV_W, sl] * x
        for d in range(1, CONV_W):
            rolled = pltpu.roll(x, d, 0)
            head = jnp.where(row8 < d, pltpu.roll(tail, d, 0), rolled[:SUBLANES, :])
            shifted = jnp.concatenate([head, rolled[SUBLANES:, :]], axis=0)
            xr = xr + cw_ref[CONV_W - 1 - d:CONV_W - d, sl] * shifted
        gates = jnp.dot(xr.astype(BF16), wg_ref[n], preferred_element_type=F32)
        r = jax.nn.sigmoid(gates[:, :LRU_BLOCK_W] + br_ref[:, sl])
        ig = jax.nn.sigmoid(gates[:, LRU_BLOCK_W:] + bi_ref[:, sl])
        log_a = -LRU_C * r * _softplus(-lam_ref[:, sl])
        a = jnp.exp(log_a)
        th = jnp.tanh(log_a)
        bt = jnp.sqrt(-2.0 * th / (1.0 - th)) * (ig * xr)
        d = 1
        while d < ts:
            keep = row >= d
            a_sh = jnp.where(keep, pltpu.roll(a, d, 0), 1.0)
            b_sh = jnp.where(keep, pltpu.roll(bt, d, 0), 0.0)
            bt = a * b_sh + bt
            a = a * a_sh
            d *= 2
        h = a * carry_ref[0:1, sl] + bt
        carry_ref[0:1, sl] = jnp.sum(jnp.where(row == ts - 1, h, 0.0), axis=0, keepdims=True)
        y = ly_ref[0, :, sl].astype(F32)
        o_ref[0, :, sl] = (h * _gelu_tanh(y)).astype(o_ref.dtype)


def _rglru(z3, conv_w, conv_b, w_gates, b_r, b_i, lru_lambda, ts=256):
    b, s, _ = z3.shape
    lx_blk, ly_blk = OFF_LX // LRU_W, OFF_LY // LRU_W
    rows_per_tail = ts // BF16_ROWS
    vec = lambda a: a.reshape(1, LRU_W)
    const2 = lambda bi, t: (0, 0)
    return pl.pallas_call(
        functools.partial(_rglru_kernel, ts=ts),
        grid=(b, s // ts),
        in_specs=[pl.BlockSpec((1, ts, LRU_W), lambda bi, t: (bi, t, lx_blk)),
                  pl.BlockSpec((1, BF16_ROWS, LRU_W),
                               lambda bi, t: (bi, jnp.maximum(t * rows_per_tail - 1, 0), lx_blk)),
                  pl.BlockSpec((1, ts, LRU_W), lambda bi, t: (bi, t, ly_blk)),
                  pl.BlockSpec((CONV_W, LRU_W), const2),
                  pl.BlockSpec((1, LRU_W), const2),
                  pl.BlockSpec((LRU_BLOCKS, LRU_BLOCK_W, 2 * LRU_BLOCK_W), lambda bi, t: (0, 0, 0)),
                  pl.BlockSpec((1, LRU_W), const2),
                  pl.BlockSpec((1, LRU_W), const2),
                  pl.BlockSpec((1, LRU_W), const2)],
        out_specs=pl.BlockSpec((1, ts, LRU_W), lambda bi, t: (bi, t, 0)),
        out_shape=jax.ShapeDtypeStruct((b, s, LRU_W), BF16),
        scratch_shapes=[pltpu.VMEM((SUBLANES, LRU_W), F32)],
        compiler_params=_params("parallel", "arbitrary"),
        name="rglru",
    )(z3, z3, z3, conv_w, vec(conv_b), w_gates, vec(b_r), vec(b_i), vec(lru_lambda))


def kernel(x, norm1_g, w_in, gate_b, lam_qk, subln_g, w_attn_proj, conv_w, conv_b, w_rgate, b_rgate,
           w_igate, b_igate, lru_lambda, w_rec_proj, w_out, norm2_g, w_up, w_down, final_g):
    bsz, seq, d = x.shape
    t = bsz * seq
    x = x.reshape(t, d)
    colscale = jnp.ones((1, C_IN), F32).at[:, OFF_Q:OFF_Q + ATTN_W].set(HEAD_DIM ** -0.5)
    for l in range(DEPTH):
        lam_init = 0.8 - 0.6 * math.exp(-0.3 * l)
        w_gates = jnp.concatenate([w_rgate[l], w_igate[l]], axis=-1).astype(BF16)
        h = _rmsnorm(x, norm1_g[l], BF16)
        z = _in_proj(h, w_in[l].astype(BF16), colscale)
        z3 = z.reshape(bsz, seq, C_IN)
        att = _attention(z3, lam_qk[l], subln_g[l], lam_init)
        rec = _rglru(z3, conv_w[l], conv_b[l], w_gates, b_rgate[l], b_igate[l], lru_lambda[l])
        m = _merge(att.reshape(t, ATTN_W), rec.reshape(t, LRU_W), w_attn_proj[l].astype(BF16),
                   w_rec_proj[l].astype(BF16), z, gate_b[l])
        x = _residual_matmul(m, w_out[l].astype(BF16), x, bm=1024, bn=512, bk=D_MODEL, name="out_proj")
        h2 = _rmsnorm(x, norm2_g[l], BF16)
        u = _up_proj(h2, w_up[l].astype(BF16))
        x = _residual_matmul(u, w_down[l].astype(BF16), x, bm=1024, bn=1024, bk=2048, name="down_proj")
    return _rmsnorm(x, final_g, F32).reshape(bsz, seq, d)
```

```python
import functools
import math

import numpy as np
import jax
import jax.numpy as jnp
from jax import lax
from jax.experimental import pallas as pl
from jax.experimental.pallas import tpu as pltpu

D_MODEL = 4096
DEPTH = 4
N_HEADS = 8
HEAD_DIM = 128
HEAD_W = 2 * HEAD_DIM
ATTN_W = N_HEADS * HEAD_W
LRU_W = 1024
LRU_BLOCKS = 8
LRU_BLOCK_W = LRU_W // LRU_BLOCKS
CONV_W = 4
LRU_C = 8.0
D_FF = 4 * D_MODEL
EPS = 1e-6
C_IN = 3 * ATTN_W + 2 * LRU_W + 2 * D_MODEL
OFF_Q, OFF_K, OFF_V = 0, ATTN_W, 2 * ATTN_W
OFF_LX = 3 * ATTN_W
OFF_LY = OFF_LX + LRU_W
OFF_GA = OFF_LY + LRU_W
OFF_GR = OFF_GA + D_MODEL

LANES = 128
SUBLANES = 8
BF16_ROWS = 16
VMEM_LIMIT = 56 * 1024 * 1024
LOG2_E = math.log2(math.e)

TILES = {
    "norm_rows": 256,
    "in_proj": (1024, 1024),
    "up_proj": (1024, 1024),
    "merge": (1024, 1024),
    "out_proj": (1024, 512, D_MODEL),
    "down_proj": (1024, 1024, 2048),
    "attn_q_rows": 256,
    "rglru_rows": 256,
}

BF16 = jnp.bfloat16
F32 = jnp.float32


def _params(*sem):
    return pltpu.CompilerParams(dimension_semantics=sem, vmem_limit_bytes=VMEM_LIMIT)


def _row_scale(r_ref):
    return r_ref[:, 0:1]


def _cast_plan(casts, grid):
    steps = math.prod(grid)

    def linear(*g):
        idx = g[0]
        for size, gi in zip(grid[1:], g[1:]):
            idx = idx * size + gi
        return idx

    in_specs, out_specs, out_shapes, args = [], [], [], []
    for w, layer in casts:
        _, k, n = w.shape
        rows = k // steps
        assert rows * steps == k and rows % BF16_ROWS == 0, (w.shape, grid)
        in_specs.append(pl.BlockSpec((None, rows, n), lambda *g, layer=layer: (layer, linear(*g), 0)))
        out_specs.append(pl.BlockSpec((rows, n), lambda *g: (linear(*g), 0)))
        out_shapes.append(jax.ShapeDtypeStruct((k, n), BF16))
        args.append(w)
    return in_specs, out_specs, out_shapes, args


def _run_casts(cast_in, cast_out):
    for src, dst in zip(cast_in, cast_out):
        dst[...] = src[...].astype(dst.dtype)


def _prep_kernel(x_ref, g_ref, xg_ref, r_ref):
    x = x_ref[...]
    r = lax.rsqrt(jnp.mean(x * x, axis=-1, keepdims=True) + EPS)
    xg_ref[...] = (x * g_ref[...]).astype(xg_ref.dtype)
    r_ref[...] = jnp.broadcast_to(r, r_ref.shape)


def _prep(x, g):
    t, d = x.shape
    bm = TILES["norm_rows"]
    return pl.pallas_call(
        _prep_kernel,
        grid=(t // bm,),
        in_specs=[pl.BlockSpec((bm, d), lambda i: (i, 0)),
                  pl.BlockSpec((1, d), lambda i: (0, 0))],
        out_specs=[pl.BlockSpec((bm, d), lambda i: (i, 0)),
                   pl.BlockSpec((bm, LANES), lambda i: (i, 0))],
        out_shape=[jax.ShapeDtypeStruct((t, d), BF16),
                   jax.ShapeDtypeStruct((t, LANES), F32)],
        compiler_params=_params("parallel"),
        name="norm_prep",
    )(x, g.reshape(1, d))


def _final_norm_kernel(x_ref, r_ref, g_ref, o_ref):
    o_ref[...] = (x_ref[...] * _row_scale(r_ref)) * g_ref[...]


def _final_norm(x, r, g):
    t, d = x.shape
    bm = TILES["norm_rows"]
    return pl.pallas_call(
        _final_norm_kernel,
        grid=(t // bm,),
        in_specs=[pl.BlockSpec((bm, d), lambda i: (i, 0)),
                  pl.BlockSpec((bm, LANES), lambda i: (i, 0)),
                  pl.BlockSpec((1, d), lambda i: (0, 0))],
        out_specs=pl.BlockSpec((bm, d), lambda i: (i, 0)),
        out_shape=jax.ShapeDtypeStruct((t, d), F32),
        compiler_params=_params("parallel"),
        name="final_norm",
    )(x, r, g.reshape(1, d))


def _normed_matmul_kernel(*refs, n_cast, has_colscale, relu2):
    a_ref, r_ref, b_ref = refs[:3]
    n_in = 3 + has_colscale
    cast_in = refs[n_in:n_in + n_cast]
    o_ref = refs[n_in + n_cast]
    cast_out = refs[n_in + n_cast + 1:]
    acc = jnp.dot(a_ref[...], b_ref[...], preferred_element_type=F32)
    y = acc * _row_scale(r_ref)
    if has_colscale:
        y = y * refs[3][...]
    if relu2:
        y = jnp.square(jnp.maximum(y, 0.0))
    o_ref[...] = y.astype(o_ref.dtype)
    _run_casts(cast_in, cast_out)


def _normed_matmul(xg, r, w, colscale, casts, relu2, name):
    t, k = xg.shape
    n = w.shape[1]
    bm, bn = TILES[name]
    grid = (t // bm, n // bn)
    c_in, c_out, c_shapes, c_args = _cast_plan(casts, grid)
    return pl.pallas_call(
        functools.partial(_normed_matmul_kernel, n_cast=len(casts), has_colscale=colscale is not None, relu2=relu2),
        grid=grid,
        in_specs=[pl.BlockSpec((bm, k), lambda i, j: (i, 0)),
                  pl.BlockSpec((bm, LANES), lambda i, j: (i, 0)),
                  pl.BlockSpec((k, bn), lambda i, j: (0, j))]
                 + ([] if colscale is None else [pl.BlockSpec((1, bn), lambda i, j: (0, j))]) + c_in,
        out_specs=[pl.BlockSpec((bm, bn), lambda i, j: (i, j))] + c_out,
        out_shape=[jax.ShapeDtypeStruct((t, n), BF16)] + c_shapes,
        compiler_params=_params("parallel", "parallel"),
        name=name,
    )(xg, r, w, *([] if colscale is None else [colscale]), *c_args)


def _merge_kernel(*refs, n_cast):
    att_ref, rec_ref, wa_ref, wr_ref, ga_ref, gr_ref, gb_ref = refs[:7]
    cast_in = refs[7:7 + n_cast]
    o_ref = refs[7 + n_cast]
    cast_out = refs[8 + n_cast:]
    y_att = jnp.dot(att_ref[...], wa_ref[...], preferred_element_type=F32)
    y_rec = jnp.dot(rec_ref[...], wr_ref[...], preferred_element_type=F32)
    g_att = jax.nn.sigmoid(ga_ref[...].astype(F32) + gb_ref[0:1, :])
    g_rec = jax.nn.sigmoid(gr_ref[...].astype(F32) + gb_ref[1:2, :])
    o_ref[...] = (g_att * y_att + g_rec * y_rec).astype(o_ref.dtype)
    _run_casts(cast_in, cast_out)


def _merge(att, rec, w_att, w_rec, z, gate_b, casts):
    t = att.shape[0]
    n = w_att.shape[1]
    bm, bn = TILES["merge"]
    grid = (t // bm, n // bn)
    ga_blk, gr_blk = OFF_GA // bn, OFF_GR // bn
    c_in, c_out, c_shapes, c_args = _cast_plan(casts, grid)
    return pl.pallas_call(
        functools.partial(_merge_kernel, n_cast=len(casts)),
        grid=grid,
        in_specs=[pl.BlockSpec((bm, ATTN_W), lambda i, j: (i, 0)),
                  pl.BlockSpec((bm, LRU_W), lambda i, j: (i, 0)),
                  pl.BlockSpec((ATTN_W, bn), lambda i, j: (0, j)),
                  pl.BlockSpec((LRU_W, bn), lambda i, j: (0, j)),
                  pl.BlockSpec((bm, bn), lambda i, j: (i, j + ga_blk)),
                  pl.BlockSpec((bm, bn), lambda i, j: (i, j + gr_blk)),
                  pl.BlockSpec((2, bn), lambda i, j: (0, j))] + c_in,
        out_specs=[pl.BlockSpec((bm, bn), lambda i, j: (i, j))] + c_out,
        out_shape=[jax.ShapeDtypeStruct((t, n), BF16)] + c_shapes,
        compiler_params=_params("parallel", "parallel"),
        name="merge",
    )(att, rec, w_att, w_rec, z, z, gate_b, *c_args)


def _residual_matmul_kernel(*refs, nj, nk, n_cast, emit_xg):
    a_ref, b_ref, x_ref = refs[:3]
    n_in = 3 + emit_xg
    g_ref = refs[3] if emit_xg else None
    cast_in = refs[n_in:n_in + n_cast]
    outs = refs[n_in + n_cast:len(refs) - 1]
    ss_ref = refs[-1]
    o_ref = outs[0]
    xg_ref = outs[1] if emit_xg else None
    r_ref = outs[1 + emit_xg]
    cast_out = outs[2 + emit_xg:]
    j, k = pl.program_id(1), pl.program_id(2)

    @pl.when(k == 0)
    def _():
        o_ref[...] = x_ref[...] + jnp.dot(a_ref[...], b_ref[...], preferred_element_type=F32)

    @pl.when(k > 0)
    def _():
        o_ref[...] += jnp.dot(a_ref[...], b_ref[...], preferred_element_type=F32)

    @pl.when(k == nk - 1)
    def _():
        xn = o_ref[...]
        if emit_xg:
            xg_ref[...] = (xn * g_ref[...]).astype(xg_ref.dtype)
        sq = xn * xn
        part = sq[:, 0:LANES]
        for c in range(1, sq.shape[1] // LANES):
            part = part + sq[:, c * LANES:(c + 1) * LANES]

        @pl.when(j == 0)
        def _():
            ss_ref[...] = part

        @pl.when(j > 0)
        def _():
            ss_ref[...] += part

        @pl.when(j == nj - 1)
        def _():
            mean_sq = jnp.sum(ss_ref[...], axis=-1, keepdims=True) * (1.0 / (nj * sq.shape[1]))
            r_ref[...] = jnp.broadcast_to(lax.rsqrt(mean_sq + EPS), r_ref.shape)

    _run_casts(cast_in, cast_out)


def _residual_matmul(a, w, x, g_next, casts, name):
    t, kdim = a.shape
    n = w.shape[1]
    bm, bn, bk = TILES[name]
    nj, nk = n // bn, kdim // bk
    grid = (t // bm, nj, nk)
    emit_xg = g_next is not None
    c_in, c_out, c_shapes, c_args = _cast_plan(casts, grid)
    tile = pl.BlockSpec((bm, bn), lambda i, j, k: (i, j))
    out_specs = [tile] + ([tile] if emit_xg else []) + [pl.BlockSpec((bm, LANES), lambda i, j, k: (i, 0))]
    out_shape = ([jax.ShapeDtypeStruct((t, n), F32)] + ([jax.ShapeDtypeStruct((t, n), BF16)] if emit_xg else [])
                 + [jax.ShapeDtypeStruct((t, LANES), F32)])
    return pl.pallas_call(
        functools.partial(_residual_matmul_kernel, nj=nj, nk=nk, n_cast=len(casts), emit_xg=emit_xg),
        grid=grid,
        in_specs=[pl.BlockSpec((bm, bk), lambda i, j, k: (i, k)),
                  pl.BlockSpec((bk, bn), lambda i, j, k: (k, j)),
                  tile]
                 + ([pl.BlockSpec((1, bn), lambda i, j, k: (0, j))] if emit_xg else []) + c_in,
        out_specs=out_specs + c_out,
        out_shape=out_shape + c_shapes,
        scratch_shapes=[pltpu.VMEM((bm, LANES), F32)],
        compiler_params=_params("parallel", "arbitrary", "arbitrary"),
        name=name,
    )(a, w, x, *([g_next.reshape(1, n)] if emit_xg else []), *c_args)


def _attn_kernel(q_ref, k_ref, v_ref, lq_ref, sg_ref, o_ref, *, lam_init, tq):
    seq = q_ref.shape[1]
    lq = lq_ref[...]
    lam = (jnp.exp(jnp.sum(lq[0:1] * lq[1:2], axis=-1, keepdims=True))
           - jnp.exp(jnp.sum(lq[2:3] * lq[3:4], axis=-1, keepdims=True)) + lam_init)
    contract_last = (((1,), (1,)), ((), ()))
    row = lax.broadcasted_iota(jnp.int32, (tq, tq), 0)
    col = lax.broadcasted_iota(jnp.int32, (tq, tq), 1)
    visible = row >= col
    for i in range(seq // tq):
        off = i * tq
        q = q_ref[0, off:off + tq, :]
        e_diag, e_past, coef = [], [], []
        for c in range(2):
            comp = slice(c * HEAD_DIM, (c + 1) * HEAD_DIM)
            s_d = lax.dot_general(q[:, comp], k_ref[0, off:off + tq, comp], contract_last,
                                  preferred_element_type=F32)
            s_d = jnp.where(visible, s_d, -jnp.inf)
            m = jnp.max(s_d, axis=-1, keepdims=True)
            if i > 0:
                s_p = lax.dot_general(q[:, comp], k_ref[0, 0:off, comp], contract_last,
                                      preferred_element_type=F32)
                m = jnp.maximum(m, jnp.max(s_p, axis=-1, keepdims=True))
                e_p = jnp.exp2(s_p - m)
                e_past.append(e_p)
            e_d = jnp.exp2(s_d - m)
            e_diag.append(e_d)
            denom = jnp.sum(e_d, axis=-1, keepdims=True)
            if i > 0:
                denom = denom + jnp.sum(e_p, axis=-1, keepdims=True)
            coef.append((1.0 if c == 0 else lam) / denom)
        w_d = e_diag[0] * coef[0] - e_diag[1] * coef[1]
        o = jnp.dot(w_d.astype(BF16), v_ref[0, off:off + tq, :], preferred_element_type=F32)
        if i > 0:
            w_p = e_past[0] * coef[0] - e_past[1] * coef[1]
            o = o + jnp.dot(w_p.astype(BF16), v_ref[0, 0:off, :], preferred_element_type=F32)
        o = o * lax.rsqrt(jnp.mean(o * o, axis=-1, keepdims=True) + EPS)
        o = (o * sg_ref[...]) * (1.0 - lam_init)
        o_ref[0, off:off + tq, :] = o.astype(o_ref.dtype)


def _attention(z3, lam_qk, subln_g, lam_init):
    b, s, _ = z3.shape
    tq = TILES["attn_q_rows"]
    kb, vb = OFF_K // HEAD_W, OFF_V // HEAD_W
    return pl.pallas_call(
        functools.partial(_attn_kernel, lam_init=lam_init, tq=tq),
        grid=(b, N_HEADS),
        in_specs=[pl.BlockSpec((1, s, HEAD_W), lambda bi, h: (bi, 0, h)),
                  pl.BlockSpec((1, s, HEAD_W), lambda bi, h: (bi, 0, h + kb)),
                  pl.BlockSpec((1, s, HEAD_W), lambda bi, h: (bi, 0, h + vb)),
                  pl.BlockSpec((4, HEAD_DIM), lambda bi, h: (0, 0)),
                  pl.BlockSpec((1, HEAD_W), lambda bi, h: (0, 0))],
        out_specs=pl.BlockSpec((1, s, HEAD_W), lambda bi, h: (bi, 0, h)),
        out_shape=jax.ShapeDtypeStruct((b, s, ATTN_W), BF16),
        compiler_params=_params("parallel", "parallel"),
        name="diff_attention",
    )(z3, z3, z3, lam_qk, subln_g.reshape(1, HEAD_W))


def _softplus(x):
    return jnp.maximum(x, 0.0) + jnp.log1p(jnp.exp(-jnp.abs(x)))


def _gelu_tanh(x):
    sqrt_2_over_pi = np.sqrt(2 / np.pi).astype(np.float32)
    cdf = 0.5 * (1.0 + jnp.tanh(sqrt_2_over_pi * (x + 0.044715 * (x * x * x))))
    return x * cdf


def _rglru_kernel(lx_ref, lxp_ref, ly_ref, cw_ref, cb_ref, wg_ref, br_ref, bi_ref, lam_ref,
                  o_ref, carry_ref, *, ts):
    t = pl.program_id(1)

    @pl.when(t == 0)
    def _():
        carry_ref[...] = jnp.zeros_like(carry_ref)

    n_tiles = ts // SUBLANES
    row8 = lax.broadcasted_iota(jnp.int32, (SUBLANES, LRU_BLOCK_W), 0)
    row_in_tile = lax.broadcasted_iota(jnp.int32, (n_tiles, SUBLANES, LRU_BLOCK_W), 1)
    for n in range(LRU_W // LRU_BLOCK_W):
        sl = slice(n * LRU_BLOCK_W, (n + 1) * LRU_BLOCK_W)
        x = lx_ref[0, :, sl].astype(F32)
        tail = lxp_ref[0, :, sl].astype(F32)[BF16_ROWS - SUBLANES:, :]
        tail = jnp.where(t > 0, tail, 0.0)
        xr = cb_ref[:, sl] + cw_ref[CONV_W - 1:CONV_W, sl] * x
        for d in range(1, CONV_W):
            rolled = pltpu.roll(x, d, 0)
            head = jnp.where(row8 < d, pltpu.roll(tail, d, 0), rolled[:SUBLANES, :])
            shifted = jnp.concatenate([head, rolled[SUBLANES:, :]], axis=0)
            xr = xr + cw_ref[CONV_W - 1 - d:CONV_W - d, sl] * shifted
        gates = jnp.dot(xr.astype(BF16), wg_ref[n], preferred_element_type=F32)
        r = jax.nn.sigmoid(gates[:, :LRU_BLOCK_W] + br_ref[:, sl])
        ig = jax.nn.sigmoid(gates[:, LRU_BLOCK_W:] + bi_ref[:, sl])
        log_a = -LRU_C * r * _softplus(-lam_ref[:, sl])
        a = jnp.exp(log_a)
        th = jnp.tanh(log_a)
        bt = jnp.sqrt(-2.0 * th / (1.0 - th)) * (ig * xr)
        a3 = a.reshape(n_tiles, SUBLANES, LRU_BLOCK_W)
        b3 = bt.reshape(n_tiles, SUBLANES, LRU_BLOCK_W)
        d = 1
        while d < SUBLANES:
            keep = row_in_tile >= d
            a_sh = jnp.where(keep, pltpu.roll(a3, d, 1), 1.0)
            b_sh = jnp.where(keep, pltpu.roll(b3, d, 1), 0.0)
            b3 = a3 * b_sh + b3
            a3 = a3 * a_sh
            d *= 2
        carry = carry_ref[0:1, sl]
        tiles = []
        for v in range(n_tiles):
            h_v = a3[v] * carry + b3[v]
            tiles.append(h_v)
            carry = h_v[SUBLANES - 1:SUBLANES, :]
        carry_ref[0:1, sl] = carry
        h = jnp.concatenate(tiles, axis=0)
        y = ly_ref[0, :, sl].astype(F32)
        o_ref[0, :, sl] = (h * _gelu_tanh(y)).astype(o_ref.dtype)


def _rglru(z3, conv_w, conv_b, w_gates, b_r, b_i, lru_lambda):
    b, s, _ = z3.shape
    ts = TILES["rglru_rows"]
    lx_blk, ly_blk = OFF_LX // LRU_W, OFF_LY // LRU_W
    rows_per_tail = ts // BF16_ROWS
    vec = lambda a: a.reshape(1, LRU_W)
    const2 = lambda bi, t: (0, 0)
    return pl.pallas_call(
        functools.partial(_rglru_kernel, ts=ts),
        grid=(b, s // ts),
        in_specs=[pl.BlockSpec((1, ts, LRU_W), lambda bi, t: (bi, t, lx_blk)),
                  pl.BlockSpec((1, BF16_ROWS, LRU_W),
                               lambda bi, t: (bi, jnp.maximum(t * rows_per_tail - 1, 0), lx_blk)),
                  pl.BlockSpec((1, ts, LRU_W), lambda bi, t: (bi, t, ly_blk)),
                  pl.BlockSpec((CONV_W, LRU_W), const2),
                  pl.BlockSpec((1, LRU_W), const2),
                  pl.BlockSpec((LRU_W // LRU_BLOCK_W, LRU_BLOCK_W, 2 * LRU_BLOCK_W), lambda bi, t: (0, 0, 0)),
                  pl.BlockSpec((1, LRU_W), const2),
                  pl.BlockSpec((1, LRU_W), const2),
                  pl.BlockSpec((1, LRU_W), const2)],
        out_specs=pl.BlockSpec((1, ts, LRU_W), lambda bi, t: (bi, t, 0)),
        out_shape=jax.ShapeDtypeStruct((b, s, LRU_W), BF16),
        scratch_shapes=[pltpu.VMEM((SUBLANES, LRU_W), F32)],
        compiler_params=_params("parallel", "arbitrary"),
        name="rglru",
    )(z3, z3, z3, conv_w, vec(conv_b), w_gates, vec(b_r), vec(b_i), vec(lru_lambda))


def kernel(x, norm1_g, w_in, gate_b, lam_qk, subln_g, w_attn_proj, conv_w, conv_b, w_rgate, b_rgate,
           w_igate, b_igate, lru_lambda, w_rec_proj, w_out, norm2_g, w_up, w_down, final_g):
    bsz, seq, d = x.shape
    t = bsz * seq
    x = x.reshape(t, d)
    q_scale = jnp.ones((1, C_IN), F32).at[:, OFF_Q:OFF_Q + ATTN_W].set(HEAD_DIM ** -0.5 * LOG2_E)
    wb_in, wb_att, wb_rec, wb_out, wb_up, wb_down = (
        w[0].astype(BF16) for w in (w_in, w_attn_proj, w_rec_proj, w_out, w_up, w_down))
    xg, r = _prep(x, norm1_g[0])
    for l in range(DEPTH):
        last = l == DEPTH - 1
        nxt = lambda *ws: [] if last else [(w, l + 1) for w in ws]
        lam_init = 0.8 - 0.6 * math.exp(-0.3 * l)
        w_gates = jnp.concatenate([w_rgate[l], w_igate[l]], axis=-1).astype(BF16)
        z, *nb_in = _normed_matmul(xg, r, wb_in, q_scale, nxt(w_in), relu2=False, name="in_proj")
        z3 = z.reshape(bsz, seq, C_IN)
        att = _attention(z3, lam_qk[l], subln_g[l], lam_init)
        rec = _rglru(z3, conv_w[l], conv_b[l], w_gates, b_rgate[l], b_igate[l], lru_lambda[l])
        m, *nb_proj = _merge(att.reshape(t, ATTN_W), rec.reshape(t, LRU_W), wb_att, wb_rec, z, gate_b[l],
                             nxt(w_attn_proj, w_rec_proj))
        x, xg, r, *nb_out = _residual_matmul(m, wb_out, x, norm2_g[l], nxt(w_out), name="out_proj")
        u, *nb_up = _normed_matmul(xg, r, wb_up, None, nxt(w_up), relu2=True, name="up_proj")
        if last:
            x, r = _residual_matmul(u, wb_down, x, None, [], name="down_proj")
        else:
            x, xg, r, *nb_down = _residual_matmul(u, wb_down, x, norm1_g[l + 1], nxt(w_down), name="down_proj")
            (wb_in,), (wb_att, wb_rec), (wb_out,), (wb_up,), (wb_down,) = nb_in, nb_proj, nb_out, nb_up, nb_down
    return _final_norm(x, r, final_g).reshape(bsz, seq, d)
```

```python
import functools
import math

import numpy as np
import jax
import jax.numpy as jnp
from jax import lax
from jax.experimental import pallas as pl
from jax.experimental.pallas import tpu as pltpu

D_MODEL = 4096
DEPTH = 4
N_HEADS = 8
HEAD_DIM = 128
HEAD_W = 2 * HEAD_DIM
ATTN_W = N_HEADS * HEAD_W
LRU_W = 1024
LRU_BLOCKS = 8
LRU_BLOCK_W = LRU_W // LRU_BLOCKS
CONV_W = 4
LRU_C = 8.0
D_FF = 4 * D_MODEL
EPS = 1e-6
C_IN = 3 * ATTN_W + 2 * LRU_W + 2 * D_MODEL
OFF_Q, OFF_K, OFF_V = 0, ATTN_W, 2 * ATTN_W
OFF_LX = 3 * ATTN_W
OFF_LY = OFF_LX + LRU_W
OFF_GA = OFF_LY + LRU_W
OFF_GR = OFF_GA + D_MODEL

LANES = 128
SUBLANES = 8
BF16_ROWS = 16
VMEM_LIMIT = 56 * 1024 * 1024
LOG2_E = math.log2(math.e)

TILES = {
    "norm_rows": 256,
    "in_proj": (1024, 1024),
    "up_proj": (1024, 1024),
    "merge": (1024, 1024),
    "out_proj": (1024, 512, D_MODEL),
    "down_proj": (1024, 1024, 2048),
    "attn_q_rows": 256,
    "rglru_rows": 256,
}

BF16 = jnp.bfloat16
F32 = jnp.float32


def _params(*sem):
    return pltpu.CompilerParams(dimension_semantics=sem, vmem_limit_bytes=VMEM_LIMIT)


def _row_scale(r_ref):
    return r_ref[:, 0:1]


def _cast_plan(casts, grid):
    steps = math.prod(grid)

    def linear(*g):
        idx = g[0]
        for size, gi in zip(grid[1:], g[1:]):
            idx = idx * size + gi
        return idx

    in_specs, out_specs, out_shapes, args = [], [], [], []
    for w, layer in casts:
        _, k, n = w.shape
        rows = k // steps
        assert rows * steps == k and rows % BF16_ROWS == 0, (w.shape, grid)
        in_specs.append(pl.BlockSpec((None, rows, n), lambda *g, layer=layer: (layer, linear(*g), 0)))
        out_specs.append(pl.BlockSpec((rows, n), lambda *g: (linear(*g), 0)))
        out_shapes.append(jax.ShapeDtypeStruct((k, n), BF16))
        args.append(w)
    return in_specs, out_specs, out_shapes, args


def _run_casts(cast_in, cast_out):
    for src, dst in zip(cast_in, cast_out):
        dst[...] = src[...].astype(dst.dtype)


def _prep_kernel(x_ref, g_ref, xg_ref, r_ref):
    x = x_ref[...]
    r = lax.rsqrt(jnp.mean(x * x, axis=-1, keepdims=True) + EPS)
    xg_ref[...] = (x * g_ref[...]).astype(xg_ref.dtype)
    r_ref[...] = jnp.broadcast_to(r, r_ref.shape)


def _prep(x, g):
    t, d = x.shape
    bm = TILES["norm_rows"]
    return pl.pallas_call(
        _prep_kernel,
        grid=(t // bm,),
        in_specs=[pl.BlockSpec((bm, d), lambda i: (i, 0)),
                  pl.BlockSpec((1, d), lambda i: (0, 0))],
        out_specs=[pl.BlockSpec((bm, d), lambda i: (i, 0)),
                   pl.BlockSpec((bm, LANES), lambda i: (i, 0))],
        out_shape=[jax.ShapeDtypeStruct((t, d), BF16),
                   jax.ShapeDtypeStruct((t, LANES), F32)],
        compiler_params=_params("parallel"),
        name="norm_prep",
    )(x, g.reshape(1, d))


def _final_norm_kernel(x_ref, r_ref, g_ref, o_ref):
    o_ref[...] = (x_ref[...] * _row_scale(r_ref)) * g_ref[...]


def _final_norm(x, r, g):
    t, d = x.shape
    bm = TILES["norm_rows"]
    return pl.pallas_call(
        _final_norm_kernel,
        grid=(t // bm,),
        in_specs=[pl.BlockSpec((bm, d), lambda i: (i, 0)),
                  pl.BlockSpec((bm, LANES), lambda i: (i, 0)),
                  pl.BlockSpec((1, d), lambda i: (0, 0))],
        out_specs=pl.BlockSpec((bm, d), lambda i: (i, 0)),
        out_shape=jax.ShapeDtypeStruct((t, d), F32),
        compiler_params=_params("parallel"),
        name="final_norm",
    )(x, r, g.reshape(1, d))


def _normed_matmul_kernel(*refs, n_cast, has_colscale, relu2):
    a_ref, r_ref, b_ref = refs[:3]
    n_in = 3 + has_colscale
    cast_in = refs[n_in:n_in + n_cast]
    o_ref = refs[n_in + n_cast]
    cast_out = refs[n_in + n_cast + 1:]
    acc = jnp.dot(a_ref[...], b_ref[...], preferred_element_type=F32)
    y = acc * _row_scale(r_ref)
    if has_colscale:
        y = y * refs[3][...]
    if relu2:
        y = jnp.square(jnp.maximum(y, 0.0))
    o_ref[...] = y.astype(o_ref.dtype)
    _run_casts(cast_in, cast_out)


def _normed_matmul(xg, r, w, colscale, casts, relu2, name):
    t, k = xg.shape
    n = w.shape[1]
    bm, bn = TILES[name]
    grid = (t // bm, n // bn)
    c_in, c_out, c_shapes, c_args = _cast_plan(casts, grid)
    return pl.pallas_call(
        functools.partial(_normed_matmul_kernel, n_cast=len(casts), has_colscale=colscale is not None, relu2=relu2),
        grid=grid,
        in_specs=[pl.BlockSpec((bm, k), lambda i, j: (i, 0)),
                  pl.BlockSpec((bm, LANES), lambda i, j: (i, 0)),
                  pl.BlockSpec((k, bn), lambda i, j: (0, j))]
                 + ([] if colscale is None else [pl.BlockSpec((1, bn), lambda i, j: (0, j))]) + c_in,
        out_specs=[pl.BlockSpec((bm, bn), lambda i, j: (i, j))] + c_out,
        out_shape=[jax.ShapeDtypeStruct((t, n), BF16)] + c_shapes,
        compiler_params=_params("parallel", "parallel"),
        name=name,
    )(xg, r, w, *([] if colscale is None else [colscale]), *c_args)


def _merge_kernel(*refs, n_cast):
    att_ref, rec_ref, wa_ref, wr_ref, ga_ref, gr_ref, gb_ref = refs[:7]
    cast_in = refs[7:7 + n_cast]
    o_ref = refs[7 + n_cast]
    cast_out = refs[8 + n_cast:]
    y_att = jnp.dot(att_ref[...], wa_ref[...], preferred_element_type=F32)
    y_rec = jnp.dot(rec_ref[...], wr_ref[...], preferred_element_type=F32)
    g_att = jax.nn.sigmoid(ga_ref[...].astype(F32) + gb_ref[0:1, :])
    g_rec = jax.nn.sigmoid(gr_ref[...].astype(F32) + gb_ref[1:2, :])
    o_ref[...] = (g_att * y_att + g_rec * y_rec).astype(o_ref.dtype)
    _run_casts(cast_in, cast_out)


def _merge(att, rec, w_att, w_rec, z, gate_b, casts):
    t = att.shape[0]
    n = w_att.shape[1]
    bm, bn = TILES["merge"]
    grid = (t // bm, n // bn)
    ga_blk, gr_blk = OFF_GA // bn, OFF_GR // bn
    c_in, c_out, c_shapes, c_args = _cast_plan(casts, grid)
    return pl.pallas_call(
        functools.partial(_merge_kernel, n_cast=len(casts)),
        grid=grid,
        in_specs=[pl.BlockSpec((bm, ATTN_W), lambda i, j: (i, 0)),
                  pl.BlockSpec((bm, LRU_W), lambda i, j: (i, 0)),
                  pl.BlockSpec((ATTN_W, bn), lambda i, j: (0, j)),
                  pl.BlockSpec((LRU_W, bn), lambda i, j: (0, j)),
                  pl.BlockSpec((bm, bn), lambda i, j: (i, j + ga_blk)),
                  pl.BlockSpec((bm, bn), lambda i, j: (i, j + gr_blk)),
                  pl.BlockSpec((2, bn), lambda i, j: (0, j))] + c_in,
        out_specs=[pl.BlockSpec((bm, bn), lambda i, j: (i, j))] + c_out,
        out_shape=[jax.ShapeDtypeStruct((t, n), BF16)] + c_shapes,
        compiler_params=_params("parallel", "parallel"),
        name="merge",
    )(att, rec, w_att, w_rec, z, z, gate_b, *c_args)


def _residual_matmul_kernel(*refs, nj, nk, n_cast, emit_xg):
    a_ref, b_ref, x_ref = refs[:3]
    n_in = 3 + emit_xg
    g_ref = refs[3] if emit_xg else None
    cast_in = refs[n_in:n_in + n_cast]
    outs = refs[n_in + n_cast:len(refs) - 1]
    ss_ref = refs[-1]
    o_ref = outs[0]
    xg_ref = outs[1] if emit_xg else None
    r_ref = outs[1 + emit_xg]
    cast_out = outs[2 + emit_xg:]
    j, k = pl.program_id(1), pl.program_id(2)

    def product():
        return jnp.dot(a_ref[...], b_ref[...], preferred_element_type=F32)

    if nk > 1:
        @pl.when(k == 0)
        def _():
            o_ref[...] = x_ref[...] + product()

        @pl.when(jnp.logical_and(k > 0, k < nk - 1))
        def _():
            o_ref[...] += product()

    @pl.when(k == nk - 1)
    def _():
        xn = (o_ref[...] if nk > 1 else x_ref[...]) + product()
        o_ref[...] = xn
        if emit_xg:
            xg_ref[...] = (xn * g_ref[...]).astype(xg_ref.dtype)
        sq = xn * xn
        part = sq[:, 0:LANES]
        for c in range(1, sq.shape[1] // LANES):
            part = part + sq[:, c * LANES:(c + 1) * LANES]

        @pl.when(j == 0)
        def _():
            ss_ref[...] = part

        @pl.when(j > 0)
        def _():
            ss_ref[...] += part

        @pl.when(j == nj - 1)
        def _():
            mean_sq = jnp.sum(ss_ref[...], axis=-1, keepdims=True) * (1.0 / (nj * sq.shape[1]))
            r_ref[...] = jnp.broadcast_to(lax.rsqrt(mean_sq + EPS), r_ref.shape)

    _run_casts(cast_in, cast_out)


def _residual_matmul(a, w, x, g_next, casts, name):
    t, kdim = a.shape
    n = w.shape[1]
    bm, bn, bk = TILES[name]
    nj, nk = n // bn, kdim // bk
    grid = (t // bm, nj, nk)
    emit_xg = g_next is not None
    c_in, c_out, c_shapes, c_args = _cast_plan(casts, grid)
    tile = pl.BlockSpec((bm, bn), lambda i, j, k: (i, j))
    out_specs = [tile] + ([tile] if emit_xg else []) + [pl.BlockSpec((bm, LANES), lambda i, j, k: (i, 0))]
    out_shape = ([jax.ShapeDtypeStruct((t, n), F32)] + ([jax.ShapeDtypeStruct((t, n), BF16)] if emit_xg else [])
                 + [jax.ShapeDtypeStruct((t, LANES), F32)])
    return pl.pallas_call(
        functools.partial(_residual_matmul_kernel, nj=nj, nk=nk, n_cast=len(casts), emit_xg=emit_xg),
        grid=grid,
        in_specs=[pl.BlockSpec((bm, bk), lambda i, j, k: (i, k)),
                  pl.BlockSpec((bk, bn), lambda i, j, k: (k, j)),
                  tile]
                 + ([pl.BlockSpec((1, bn), lambda i, j, k: (0, j))] if emit_xg else []) + c_in,
        out_specs=out_specs + c_out,
        out_shape=out_shape + c_shapes,
        scratch_shapes=[pltpu.VMEM((bm, LANES), F32)],
        compiler_params=_params("parallel", "arbitrary", "arbitrary"),
        name=name,
    )(a, w, x, *([g_next.reshape(1, n)] if emit_xg else []), *c_args)


def _attn_kernel(q_ref, k_ref, v_ref, lq_ref, sg_ref, o_ref, *, lam_init, tq):
    seq = q_ref.shape[1]
    lq = lq_ref[...]
    lam = (jnp.exp(jnp.sum(lq[0:1] * lq[1:2], axis=-1, keepdims=True))
           - jnp.exp(jnp.sum(lq[2:3] * lq[3:4], axis=-1, keepdims=True)) + lam_init)
    contract_last = (((1,), (1,)), ((), ()))
    row = lax.broadcasted_iota(jnp.int32, (tq, tq), 0)
    col = lax.broadcasted_iota(jnp.int32, (tq, tq), 1)
    visible = row >= col
    def scores(i, c):
        off = i * tq
        comp = slice(c * HEAD_DIM, (c + 1) * HEAD_DIM)
        q = q_ref[0, off:off + tq, comp]
        s_d = lax.dot_general(q, k_ref[0, off:off + tq, comp], contract_last, preferred_element_type=F32)
        s_d = jnp.where(visible, s_d, -jnp.inf)
        s_p = None
        if i > 0:
            s_p = lax.dot_general(q, k_ref[0, 0:off, comp], contract_last, preferred_element_type=F32)
        return s_d, s_p

    def numerators(i, c, sc):
        s_d, s_p = sc
        m = jnp.max(s_d, axis=-1, keepdims=True)
        e_p = None
        if i > 0:
            m = jnp.maximum(m, jnp.max(s_p, axis=-1, keepdims=True))
            e_p = jnp.exp2(s_p - m)
        e_d = jnp.exp2(s_d - m)
        denom = jnp.sum(e_d, axis=-1, keepdims=True)
        if i > 0:
            denom = denom + jnp.sum(e_p, axis=-1, keepdims=True)
        return e_d, e_p, (1.0 if c == 0 else lam) / denom

    def weights(i, n0, n1):
        w_d = (n0[0] * n0[2] - n1[0] * n1[2]).astype(BF16)
        w_p = (n0[1] * n0[2] - n1[1] * n1[2]).astype(BF16) if i > 0 else None
        return w_d, w_p

    def finish(i, w):
        off = i * tq
        w_d, w_p = w
        o = jnp.dot(w_d, v_ref[0, off:off + tq, :], preferred_element_type=F32)
        if i > 0:
            o = o + jnp.dot(w_p, v_ref[0, 0:off, :], preferred_element_type=F32)
        o = o * lax.rsqrt(jnp.mean(o * o, axis=-1, keepdims=True) + EPS)
        o = (o * sg_ref[...]) * (1.0 - lam_init)
        o_ref[0, off:off + tq, :] = o.astype(o_ref.dtype)

    n_blocks = seq // tq
    sc = {(0, c): scores(0, c) for c in range(2)}
    w = {}
    for i in range(n_blocks + 1):
        num = []
        for c in range(2):
            if i + 1 < n_blocks:
                sc[i + 1, c] = scores(i + 1, c)
            if i < n_blocks:
                num.append(numerators(i, c, sc.pop((i, c))))
        if i >= 1:
            finish(i - 1, w.pop(i - 1))
        if i < n_blocks:
            w[i] = weights(i, *num)


def _attention(z3, lam_qk, subln_g, lam_init):
    b, s, _ = z3.shape
    tq = TILES["attn_q_rows"]
    kb, vb = OFF_K // HEAD_W, OFF_V // HEAD_W
    return pl.pallas_call(
        functools.partial(_attn_kernel, lam_init=lam_init, tq=tq),
        grid=(b, N_HEADS),
        in_specs=[pl.BlockSpec((1, s, HEAD_W), lambda bi, h: (bi, 0, h)),
                  pl.BlockSpec((1, s, HEAD_W), lambda bi, h: (bi, 0, h + kb)),
                  pl.BlockSpec((1, s, HEAD_W), lambda bi, h: (bi, 0, h + vb)),
                  pl.BlockSpec((4, HEAD_DIM), lambda bi, h: (0, 0)),
                  pl.BlockSpec((1, HEAD_W), lambda bi, h: (0, 0))],
        out_specs=pl.BlockSpec((1, s, HEAD_W), lambda bi, h: (bi, 0, h)),
        out_shape=jax.ShapeDtypeStruct((b, s, ATTN_W), BF16),
        compiler_params=_params("parallel", "parallel"),
        name="diff_attention",
    )(z3, z3, z3, lam_qk, subln_g.reshape(1, HEAD_W))


def _softplus(x):
    return jnp.maximum(x, 0.0) + jnp.log1p(jnp.exp(-jnp.abs(x)))


def _gelu_tanh(x):
    sqrt_2_over_pi = np.sqrt(2 / np.pi).astype(np.float32)
    cdf = 0.5 * (1.0 + jnp.tanh(sqrt_2_over_pi * (x + 0.044715 * (x * x * x))))
    return x * cdf


def _rglru_kernel(lx_ref, lxp_ref, ly_ref, cw_ref, cb_ref, wg_ref, br_ref, bi_ref, lam_ref,
                  o_ref, carry_ref, *, ts):
    t = pl.program_id(1)

    @pl.when(t == 0)
    def _():
        carry_ref[...] = jnp.zeros_like(carry_ref)

    n_tiles = ts // SUBLANES
    row8 = lax.broadcasted_iota(jnp.int32, (SUBLANES, LRU_BLOCK_W), 0)
    row_in_tile = lax.broadcasted_iota(jnp.int32, (n_tiles, SUBLANES, LRU_BLOCK_W), 1)
    for n in range(LRU_W // LRU_BLOCK_W):
        sl = slice(n * LRU_BLOCK_W, (n + 1) * LRU_BLOCK_W)
        x = lx_ref[0, :, sl].astype(F32)
        tail = lxp_ref[0, :, sl].astype(F32)[BF16_ROWS - SUBLANES:, :]
        tail = jnp.where(t > 0, tail, 0.0)
        xr = cb_ref[:, sl] + cw_ref[CONV_W - 1:CONV_W, sl] * x
        for d in range(1, CONV_W):
            rolled = pltpu.roll(x, d, 0)
            head = jnp.where(row8 < d, pltpu.roll(tail, d, 0), rolled[:SUBLANES, :])
            shifted = jnp.concatenate([head, rolled[SUBLANES:, :]], axis=0)
            xr = xr + cw_ref[CONV_W - 1 - d:CONV_W - d, sl] * shifted
        gates = jnp.dot(xr.astype(BF16), wg_ref[n], preferred_element_type=F32)
        r = jax.nn.sigmoid(gates[:, :LRU_BLOCK_W] + br_ref[:, sl])
        ig = jax.nn.sigmoid(gates[:, LRU_BLOCK_W:] + bi_ref[:, sl])
        log_a = -LRU_C * r * _softplus(-lam_ref[:, sl])
        a = jnp.exp(log_a)
        th = jnp.tanh(log_a)
        bt = jnp.sqrt(-2.0 * th / (1.0 - th)) * (ig * xr)
        a3 = a.reshape(n_tiles, SUBLANES, LRU_BLOCK_W)
        b3 = bt.reshape(n_tiles, SUBLANES, LRU_BLOCK_W)
        d = 1
        while d < SUBLANES:
            keep = row_in_tile >= d
            a_sh = jnp.where(keep, pltpu.roll(a3, d, 1), 1.0)
            b_sh = jnp.where(keep, pltpu.roll(b3, d, 1), 0.0)
            b3 = a3 * b_sh + b3
            a3 = a3 * a_sh
            d *= 2
        carry = carry_ref[0:1, sl]
        tiles = []
        for v in range(n_tiles):
            h_v = a3[v] * carry + b3[v]
            tiles.append(h_v)
            carry = h_v[SUBLANES - 1:SUBLANES, :]
        carry_ref[0:1, sl] = carry
        h = jnp.concatenate(tiles, axis=0)
        y = ly_ref[0, :, sl].astype(F32)
        o_ref[0, :, sl] = (h * _gelu_tanh(y)).astype(o_ref.dtype)


def _rglru(z3, conv_w, conv_b, w_gates, b_r, b_i, lru_lambda):
    b, s, _ = z3.shape
    ts = TILES["rglru_rows"]
    lx_blk, ly_blk = OFF_LX // LRU_W, OFF_LY // LRU_W
    rows_per_tail = ts // BF16_ROWS
    vec = lambda a: a.reshape(1, LRU_W)
    const2 = lambda bi, t: (0, 0)
    return pl.pallas_call(
        functools.partial(_rglru_kernel, ts=ts),
        grid=(b, s // ts),
        in_specs=[pl.BlockSpec((1, ts, LRU_W), lambda bi, t: (bi, t, lx_blk)),
                  pl.BlockSpec((1, BF16_ROWS, LRU_W),
                               lambda bi, t: (bi, jnp.maximum(t * rows_per_tail - 1, 0), lx_blk)),
                  pl.BlockSpec((1, ts, LRU_W), lambda bi, t: (bi, t, ly_blk)),
                  pl.BlockSpec((CONV_W, LRU_W), const2),
                  pl.BlockSpec((1, LRU_W), const2),
                  pl.BlockSpec((LRU_W // LRU_BLOCK_W, LRU_BLOCK_W, 2 * LRU_BLOCK_W), lambda bi, t: (0, 0, 0)),
                  pl.BlockSpec((1, LRU_W), const2),
                  pl.BlockSpec((1, LRU_W), const2),
                  pl.BlockSpec((1, LRU_W), const2)],
        out_specs=pl.BlockSpec((1, ts, LRU_W), lambda bi, t: (bi, t, 0)),
        out_shape=jax.ShapeDtypeStruct((b, s, LRU_W), BF16),
        scratch_shapes=[pltpu.VMEM((SUBLANES, LRU_W), F32)],
        compiler_params=_params("parallel", "arbitrary"),
        name="rglru",
    )(z3, z3, z3, conv_w, vec(conv_b), w_gates, vec(b_r), vec(b_i), vec(lru_lambda))


def kernel(x, norm1_g, w_in, gate_b, lam_qk, subln_g, w_attn_proj, conv_w, conv_b, w_rgate, b_rgate,
           w_igate, b_igate, lru_lambda, w_rec_proj, w_out, norm2_g, w_up, w_down, final_g):
    bsz, seq, d = x.shape
    t = bsz * seq
    x = x.reshape(t, d)
    q_scale = jnp.ones((1, C_IN), F32).at[:, OFF_Q:OFF_Q + ATTN_W].set(HEAD_DIM ** -0.5 * LOG2_E)
    wb_in, wb_att, wb_rec = (w[0].astype(BF16) for w in (w_in, w_attn_proj, w_rec_proj))
    xg, r = _prep(x, norm1_g[0])
    for l in range(DEPTH):
        last = l == DEPTH - 1
        nxt = lambda *ws: [] if last else [(w, l + 1) for w in ws]
        lam_init = 0.8 - 0.6 * math.exp(-0.3 * l)
        w_gates = jnp.concatenate([w_rgate[l], w_igate[l]], axis=-1).astype(BF16)
        same_layer = [(w, 0) for w in (w_out, w_up, w_down)] if l == 0 else []
        z, *converted = _normed_matmul(xg, r, wb_in, q_scale, same_layer + nxt(w_in), relu2=False, name="in_proj")
        if l == 0:
            wb_out, wb_up, wb_down = converted[:3]
        nb_in = converted[len(same_layer):]
        z3 = z.reshape(bsz, seq, C_IN)
        att = _attention(z3, lam_qk[l], subln_g[l], lam_init)
        rec = _rglru(z3, conv_w[l], conv_b[l], w_gates, b_rgate[l], b_igate[l], lru_lambda[l])
        m, *nb_proj = _merge(att.reshape(t, ATTN_W), rec.reshape(t, LRU_W), wb_att, wb_rec, z, gate_b[l],
                             nxt(w_attn_proj, w_rec_proj))
        x, xg, r, *nb_out = _residual_matmul(m, wb_out, x, norm2_g[l], nxt(w_out), name="out_proj")
        u, *nb_up = _normed_matmul(xg, r, wb_up, None, nxt(w_up), relu2=True, name="up_proj")
        if last:
            x, r = _residual_matmul(u, wb_down, x, None, [], name="down_proj")
        else:
            x, xg, r, *nb_down = _residual_matmul(u, wb_down, x, norm1_g[l + 1], nxt(w_down), name="down_proj")
            (wb_in,), (wb_att, wb_rec), (wb_out,), (wb_up,), (wb_down,) = nb_in, nb_proj, nb_out, nb_up, nb_down
    return _final_norm(x, r, final_g).reshape(bsz, seq, d)
```

```python
import functools
import math

import numpy as np
import jax
import jax.numpy as jnp
from jax import lax
from jax.experimental import pallas as pl
from jax.experimental.pallas import tpu as pltpu

D_MODEL = 4096
DEPTH = 4
N_HEADS = 8
HEAD_DIM = 128
HEAD_W = 2 * HEAD_DIM
ATTN_W = N_HEADS * HEAD_W
LRU_W = 1024
LRU_BLOCKS = 8
LRU_BLOCK_W = LRU_W // LRU_BLOCKS
CONV_W = 4
LRU_C = 8.0
D_FF = 4 * D_MODEL
EPS = 1e-6
C_IN = 3 * ATTN_W + 2 * LRU_W + 2 * D_MODEL
OFF_Q, OFF_K, OFF_V = 0, ATTN_W, 2 * ATTN_W
OFF_LX = 3 * ATTN_W
OFF_LY = OFF_LX + LRU_W
OFF_GA = OFF_LY + LRU_W
OFF_GR = OFF_GA + D_MODEL

LANES = 128
SUBLANES = 8
BF16_ROWS = 16
VMEM_LIMIT = 56 * 1024 * 1024
LOG2_E = math.log2(math.e)

TILES = {
    "norm_rows": 256,
    "in_proj": (1024, 1024),
    "up_proj": (1024, 1024),
    "merge": (1024, 1024),
    "out_proj": (1024, 512, D_MODEL),
    "down_proj": (1024, 1024, 2048),
    "attn_q_rows": 256,
    "rglru_rows": 256,
}

BF16 = jnp.bfloat16
F32 = jnp.float32


def _params(*sem):
    return pltpu.CompilerParams(dimension_semantics=sem, vmem_limit_bytes=VMEM_LIMIT)


def _row_scale(r_ref):
    return r_ref[:, 0:1]


def _cast_plan(casts, grid):
    steps = math.prod(grid)

    def linear(*g):
        idx = g[0]
        for size, gi in zip(grid[1:], g[1:]):
            idx = idx * size + gi
        return idx

    in_specs, out_specs, out_shapes, args = [], [], [], []
    for w, layer in casts:
        _, k, n = w.shape
        rows = k // steps
        assert rows * steps == k and rows % BF16_ROWS == 0, (w.shape, grid)
        in_specs.append(pl.BlockSpec((None, rows, n), lambda *g, layer=layer: (layer, linear(*g), 0)))
        out_specs.append(pl.BlockSpec((rows, n), lambda *g: (linear(*g), 0)))
        out_shapes.append(jax.ShapeDtypeStruct((k, n), BF16))
        args.append(w)
    return in_specs, out_specs, out_shapes, args


def _run_casts(cast_in, cast_out):
    for src, dst in zip(cast_in, cast_out):
        dst[...] = src[...].astype(dst.dtype)


def _prep_kernel(x_ref, g_ref, xg_ref, r_ref):
    x = x_ref[...]
    r = lax.rsqrt(jnp.mean(x * x, axis=-1, keepdims=True) + EPS)
    xg_ref[...] = (x * g_ref[...]).astype(xg_ref.dtype)
    r_ref[...] = jnp.broadcast_to(r, r_ref.shape)


def _prep(x, g):
    t, d = x.shape
    bm = TILES["norm_rows"]
    return pl.pallas_call(
        _prep_kernel,
        grid=(t // bm,),
        in_specs=[pl.BlockSpec((bm, d), lambda i: (i, 0)),
                  pl.BlockSpec((1, d), lambda i: (0, 0))],
        out_specs=[pl.BlockSpec((bm, d), lambda i: (i, 0)),
                   pl.BlockSpec((bm, LANES), lambda i: (i, 0))],
        out_shape=[jax.ShapeDtypeStruct((t, d), BF16),
                   jax.ShapeDtypeStruct((t, LANES), F32)],
        compiler_params=_params("parallel"),
        name="norm_prep",
    )(x, g.reshape(1, d))


def _final_norm_kernel(x_ref, r_ref, g_ref, o_ref):
    o_ref[...] = (x_ref[...] * _row_scale(r_ref)) * g_ref[...]


def _final_norm(x, r, g):
    t, d = x.shape
    bm = TILES["norm_rows"]
    return pl.pallas_call(
        _final_norm_kernel,
        grid=(t // bm,),
        in_specs=[pl.BlockSpec((bm, d), lambda i: (i, 0)),
                  pl.BlockSpec((bm, LANES), lambda i: (i, 0)),
                  pl.BlockSpec((1, d), lambda i: (0, 0))],
        out_specs=pl.BlockSpec((bm, d), lambda i: (i, 0)),
        out_shape=jax.ShapeDtypeStruct((t, d), F32),
        compiler_params=_params("parallel"),
        name="final_norm",
    )(x, r, g.reshape(1, d))


def _up_proj_kernel(*refs, n_cast):
    a_ref, r_ref, b_ref = refs[:3]
    cast_in = refs[3:3 + n_cast]
    o_ref = refs[3 + n_cast]
    cast_out = refs[4 + n_cast:]
    acc = jnp.dot(a_ref[...], b_ref[...], preferred_element_type=F32)
    o_ref[...] = jnp.square(jnp.maximum(acc * _row_scale(r_ref), 0.0)).astype(o_ref.dtype)
    _run_casts(cast_in, cast_out)


def _up_proj(xg, r, w, casts):
    t, k = xg.shape
    n = w.shape[1]
    bm, bn = TILES["up_proj"]
    grid = (t // bm, n // bn)
    c_in, c_out, c_shapes, c_args = _cast_plan(casts, grid)
    return pl.pallas_call(
        functools.partial(_up_proj_kernel, n_cast=len(casts)),
        grid=grid,
        in_specs=[pl.BlockSpec((bm, k), lambda i, j: (i, 0)),
                  pl.BlockSpec((bm, LANES), lambda i, j: (i, 0)),
                  pl.BlockSpec((k, bn), lambda i, j: (0, j))] + c_in,
        out_specs=[pl.BlockSpec((bm, bn), lambda i, j: (i, j))] + c_out,
        out_shape=[jax.ShapeDtypeStruct((t, n), BF16)] + c_shapes,
        compiler_params=_params("parallel", "parallel"),
        name="up_proj",
    )(xg, r, w, *c_args)


def _merge_kernel(*refs, n_cast):
    att_ref, rec_ref, wa_ref, wr_ref, ga_ref, gr_ref, gb_ref = refs[:7]
    cast_in = refs[7:7 + n_cast]
    o_ref = refs[7 + n_cast]
    cast_out = refs[8 + n_cast:]
    y_att = jnp.dot(att_ref[...], wa_ref[...], preferred_element_type=F32)
    y_rec = jnp.dot(rec_ref[...], wr_ref[...], preferred_element_type=F32)
    g_att = jax.nn.sigmoid(ga_ref[...].astype(F32) + gb_ref[0:1, :])
    g_rec = jax.nn.sigmoid(gr_ref[...].astype(F32) + gb_ref[1:2, :])
    o_ref[...] = (g_att * y_att + g_rec * y_rec).astype(o_ref.dtype)
    _run_casts(cast_in, cast_out)


def _merge(att, rec, w_att, w_rec, z, gate_b, casts):
    t = att.shape[0]
    n = w_att.shape[1]
    bm, bn = TILES["merge"]
    grid = (t // bm, n // bn)
    ga_blk, gr_blk = OFF_GA // bn, OFF_GR // bn
    c_in, c_out, c_shapes, c_args = _cast_plan(casts, grid)
    return pl.pallas_call(
        functools.partial(_merge_kernel, n_cast=len(casts)),
        grid=grid,
        in_specs=[pl.BlockSpec((bm, ATTN_W), lambda i, j: (i, 0)),
                  pl.BlockSpec((bm, LRU_W), lambda i, j: (i, 0)),
                  pl.BlockSpec((ATTN_W, bn), lambda i, j: (0, j)),
                  pl.BlockSpec((LRU_W, bn), lambda i, j: (0, j)),
                  pl.BlockSpec((bm, bn), lambda i, j: (i, j + ga_blk)),
                  pl.BlockSpec((bm, bn), lambda i, j: (i, j + gr_blk)),
                  pl.BlockSpec((2, bn), lambda i, j: (0, j))] + c_in,
        out_specs=[pl.BlockSpec((bm, bn), lambda i, j: (i, j))] + c_out,
        out_shape=[jax.ShapeDtypeStruct((t, n), BF16)] + c_shapes,
        compiler_params=_params("parallel", "parallel"),
        name="merge",
    )(att, rec, w_att, w_rec, z, z, gate_b, *c_args)


def _residual_matmul_kernel(*refs, nj, nk, n_cast, emit_xg):
    a_ref, b_ref, x_ref = refs[:3]
    n_in = 3 + emit_xg
    g_ref = refs[3] if emit_xg else None
    cast_in = refs[n_in:n_in + n_cast]
    outs = refs[n_in + n_cast:len(refs) - 1]
    ss_ref = refs[-1]
    o_ref = outs[0]
    xg_ref = outs[1] if emit_xg else None
    r_ref = outs[1 + emit_xg]
    cast_out = outs[2 + emit_xg:]
    j, k = pl.program_id(1), pl.program_id(2)

    def product():
        return jnp.dot(a_ref[...], b_ref[...], preferred_element_type=F32)

    if nk > 1:
        @pl.when(k == 0)
        def _():
            o_ref[...] = x_ref[...] + product()

        @pl.when(jnp.logical_and(k > 0, k < nk - 1))
        def _():
            o_ref[...] += product()

    @pl.when(k == nk - 1)
    def _():
        xn = (o_ref[...] if nk > 1 else x_ref[...]) + product()
        o_ref[...] = xn
        if emit_xg:
            xg_ref[...] = (xn * g_ref[...]).astype(xg_ref.dtype)
        sq = xn * xn
        part = sq[:, 0:LANES]
        for c in range(1, sq.shape[1] // LANES):
            part = part + sq[:, c * LANES:(c + 1) * LANES]

        @pl.when(j == 0)
        def _():
            ss_ref[...] = part

        @pl.when(j > 0)
        def _():
            ss_ref[...] += part

        @pl.when(j == nj - 1)
        def _():
            mean_sq = jnp.sum(ss_ref[...], axis=-1, keepdims=True) * (1.0 / (nj * sq.shape[1]))
            r_ref[...] = jnp.broadcast_to(lax.rsqrt(mean_sq + EPS), r_ref.shape)

    _run_casts(cast_in, cast_out)


def _residual_matmul(a, w, x, g_next, casts, name):
    t, kdim = a.shape
    n = w.shape[1]
    bm, bn, bk = TILES[name]
    nj, nk = n // bn, kdim // bk
    grid = (t // bm, nj, nk)
    emit_xg = g_next is not None
    c_in, c_out, c_shapes, c_args = _cast_plan(casts, grid)
    tile = pl.BlockSpec((bm, bn), lambda i, j, k: (i, j))
    out_specs = [tile] + ([tile] if emit_xg else []) + [pl.BlockSpec((bm, LANES), lambda i, j, k: (i, 0))]
    out_shape = ([jax.ShapeDtypeStruct((t, n), F32)] + ([jax.ShapeDtypeStruct((t, n), BF16)] if emit_xg else [])
                 + [jax.ShapeDtypeStruct((t, LANES), F32)])
    return pl.pallas_call(
        functools.partial(_residual_matmul_kernel, nj=nj, nk=nk, n_cast=len(casts), emit_xg=emit_xg),
        grid=grid,
        in_specs=[pl.BlockSpec((bm, bk), lambda i, j, k: (i, k)),
                  pl.BlockSpec((bk, bn), lambda i, j, k: (k, j)),
                  tile]
                 + ([pl.BlockSpec((1, bn), lambda i, j, k: (0, j))] if emit_xg else []) + c_in,
        out_specs=out_specs + c_out,
        out_shape=out_shape + c_shapes,
        scratch_shapes=[pltpu.VMEM((bm, LANES), F32)],
        compiler_params=_params("parallel", "arbitrary", "arbitrary"),
        name=name,
    )(a, w, x, *([g_next.reshape(1, n)] if emit_xg else []), *c_args)


def _attn_kernel(q_ref, k_ref, v_ref, lq_ref, sg_ref, o_ref, *, lam_init, tq):
    seq = q_ref.shape[1]
    lq = lq_ref[...]
    lam = (jnp.exp(jnp.sum(lq[0:1] * lq[1:2], axis=-1, keepdims=True))
           - jnp.exp(jnp.sum(lq[2:3] * lq[3:4], axis=-1, keepdims=True)) + lam_init)
    contract_last = (((1,), (1,)), ((), ()))
    row = lax.broadcasted_iota(jnp.int32, (tq, tq), 0)
    col = lax.broadcasted_iota(jnp.int32, (tq, tq), 1)
    visible = row >= col
    def scores(i, c):
        off = i * tq
        comp = slice(c * HEAD_DIM, (c + 1) * HEAD_DIM)
        q = q_ref[0, off:off + tq, comp]
        s_d = lax.dot_general(q, k_ref[0, off:off + tq, comp], contract_last, preferred_element_type=F32)
        s_d = jnp.where(visible, s_d, -jnp.inf)
        s_p = None
        if i > 0:
            s_p = lax.dot_general(q, k_ref[0, 0:off, comp], contract_last, preferred_element_type=F32)
        return s_d, s_p

    def numerators(i, c, sc):
        s_d, s_p = sc
        m = jnp.max(s_d, axis=-1, keepdims=True)
        e_p = None
        if i > 0:
            m = jnp.maximum(m, jnp.max(s_p, axis=-1, keepdims=True))
            e_p = jnp.exp2(s_p - m)
        e_d = jnp.exp2(s_d - m)
        denom = jnp.sum(e_d, axis=-1, keepdims=True)
        if i > 0:
            denom = denom + jnp.sum(e_p, axis=-1, keepdims=True)
        return e_d, e_p, (1.0 if c == 0 else lam) / denom

    def weights(i, n0, n1):
        w_d = (n0[0] * n0[2] - n1[0] * n1[2]).astype(BF16)
        w_p = (n0[1] * n0[2] - n1[1] * n1[2]).astype(BF16) if i > 0 else None
        return w_d, w_p

    def finish(i, w):
        off = i * tq
        w_d, w_p = w
        o = jnp.dot(w_d, v_ref[0, off:off + tq, :], preferred_element_type=F32)
        if i > 0:
            o = o + jnp.dot(w_p, v_ref[0, 0:off, :], preferred_element_type=F32)
        o = o * lax.rsqrt(jnp.mean(o * o, axis=-1, keepdims=True) + EPS)
        o = (o * sg_ref[...]) * (1.0 - lam_init)
        o_ref[0, off:off + tq, :] = o.astype(o_ref.dtype)

    n_blocks = seq // tq
    sc = {(0, c): scores(0, c) for c in range(2)}
    w = {}
    for i in range(n_blocks + 1):
        num = []
        for c in range(2):
            if i + 1 < n_blocks:
                sc[i + 1, c] = scores(i + 1, c)
            if i < n_blocks:
                num.append(numerators(i, c, sc.pop((i, c))))
        if i >= 1:
            finish(i - 1, w.pop(i - 1))
        if i < n_blocks:
            w[i] = weights(i, *num)


def _attention(z3, lam_qk, subln_g, lam_init):
    b, s, _ = z3.shape
    tq = TILES["attn_q_rows"]
    kb, vb = OFF_K // HEAD_W, OFF_V // HEAD_W
    return pl.pallas_call(
        functools.partial(_attn_kernel, lam_init=lam_init, tq=tq),
        grid=(b, N_HEADS),
        in_specs=[pl.BlockSpec((1, s, HEAD_W), lambda bi, h: (bi, 0, h)),
                  pl.BlockSpec((1, s, HEAD_W), lambda bi, h: (bi, 0, h + kb)),
                  pl.BlockSpec((1, s, HEAD_W), lambda bi, h: (bi, 0, h + vb)),
                  pl.BlockSpec((4, HEAD_DIM), lambda bi, h: (0, 0)),
                  pl.BlockSpec((1, HEAD_W), lambda bi, h: (0, 0))],
        out_specs=pl.BlockSpec((1, s, HEAD_W), lambda bi, h: (bi, 0, h)),
        out_shape=jax.ShapeDtypeStruct((b, s, ATTN_W), BF16),
        compiler_params=_params("parallel", "parallel"),
        name="diff_attention",
    )(z3, z3, z3, lam_qk, subln_g.reshape(1, HEAD_W))


def _softplus(x):
    return jnp.maximum(x, 0.0) + jnp.log1p(jnp.exp(-jnp.abs(x)))


def _gelu_tanh(x):
    sqrt_2_over_pi = np.sqrt(2 / np.pi).astype(np.float32)
    cdf = 0.5 * (1.0 + jnp.tanh(sqrt_2_over_pi * (x + 0.044715 * (x * x * x))))
    return x * cdf


def _rglru_lanes(x, tail, y, carry, cw, cb, wg, b_r, b_i, lam, row8, row_in_tile):
    ts = x.shape[0]
    n_tiles = ts // SUBLANES
    xr = cb + cw[CONV_W - 1:CONV_W] * x
    for d in range(1, CONV_W):
        rolled = pltpu.roll(x, d, 0)
        head = jnp.where(row8 < d, pltpu.roll(tail, d, 0), rolled[:SUBLANES, :])
        shifted = jnp.concatenate([head, rolled[SUBLANES:, :]], axis=0)
        xr = xr + cw[CONV_W - 1 - d:CONV_W - d] * shifted
    gates = jnp.dot(xr.astype(BF16), wg, preferred_element_type=F32)
    r = jax.nn.sigmoid(gates[:, :LRU_BLOCK_W] + b_r)
    ig = jax.nn.sigmoid(gates[:, LRU_BLOCK_W:] + b_i)
    log_a = -LRU_C * r * _softplus(-lam)
    a = jnp.exp(log_a)
    th = jnp.tanh(log_a)
    bt = jnp.sqrt(-2.0 * th / (1.0 - th)) * (ig * xr)
    a3 = a.reshape(n_tiles, SUBLANES, LRU_BLOCK_W)
    b3 = bt.reshape(n_tiles, SUBLANES, LRU_BLOCK_W)
    d = 1
    while d < SUBLANES:
        keep = row_in_tile >= d
        a_sh = jnp.where(keep, pltpu.roll(a3, d, 1), 1.0)
        b_sh = jnp.where(keep, pltpu.roll(b3, d, 1), 0.0)
        b3 = a3 * b_sh + b3
        a3 = a3 * a_sh
        d *= 2
    tiles = []
    for v in range(n_tiles):
        h_v = a3[v] * carry + b3[v]
        tiles.append(h_v)
        carry = h_v[SUBLANES - 1:SUBLANES, :]
    h = jnp.concatenate(tiles, axis=0)
    return h * _gelu_tanh(y), carry


def _in_proj_kernel(*refs, n_cast, ts, blocks_per_seq):
    xg_ref, r_ref, w_ref, cs_ref, cw_ref, cb_ref, wg_ref, br_ref, bi_ref, lam_ref = refs[:10]
    cast_in = refs[10:10 + n_cast]
    z_ref, rec_ref = refs[10 + n_cast:12 + n_cast]
    cast_out = refs[12 + n_cast:len(refs) - 4]
    lx_s, ly_s, tail_s, carry_s = refs[-4:]
    i, j = pl.program_id(0), pl.program_id(1)
    bm, bn = z_ref.shape
    n_sub = bm // ts
    n_lane_blocks = LRU_W // LRU_BLOCK_W
    chunk = bm // n_lane_blocks
    j_lx, j_ly = OFF_LX // bn, OFF_LY // bn
    j_rg = j_ly + 1

    def project(rows):
        acc = jnp.dot(xg_ref[rows, :], w_ref[...], preferred_element_type=F32)
        return ((acc * r_ref[rows, 0:1]) * cs_ref[...]).astype(z_ref.dtype)

    @pl.when(jnp.logical_and(i == 0, j == 0))
    def _():
        lx_s[bm - BF16_ROWS:bm, :] = jnp.zeros((BF16_ROWS, LRU_W), lx_s.dtype)

    runs_rglru = jnp.logical_and(j >= j_rg, j < j_rg + n_sub)

    @pl.when(jnp.logical_not(runs_rglru))
    def _():
        z = project(slice(None))
        z_ref[...] = z

        @pl.when(j == j_lx)
        def _():
            tail_s[...] = lx_s[bm - BF16_ROWS:bm, :]
            lx_s[...] = z

        @pl.when(j == j_ly)
        def _():
            ly_s[...] = z

    @pl.when(runs_rglru)
    def _():
        sub = j - j_rg
        base = pl.multiple_of(sub * ts, ts)
        before = pl.multiple_of(jnp.maximum(base - BF16_ROWS, 0), BF16_ROWS)
        seq_start = jnp.logical_and(i % blocks_per_seq == 0, sub == 0)
        row8 = lax.broadcasted_iota(jnp.int32, (SUBLANES, LRU_BLOCK_W), 0)
        row_in_tile = lax.broadcasted_iota(jnp.int32, (ts // SUBLANES, SUBLANES, LRU_BLOCK_W), 1)
        for n in range(n_lane_blocks):
            rows = slice(n * chunk, (n + 1) * chunk)
            z_ref[rows, :] = project(rows)
            sl = slice(n * LRU_BLOCK_W, (n + 1) * LRU_BLOCK_W)
            x = lx_s[pl.ds(base, ts), sl].astype(F32)
            prev = jnp.where(sub > 0, lx_s[pl.ds(before, BF16_ROWS), sl], tail_s[:, sl]).astype(F32)
            tail = jnp.where(seq_start, 0.0, prev[BF16_ROWS - SUBLANES:, :])
            carry = jnp.where(seq_start, 0.0, carry_s[0:1, sl])
            y = ly_s[pl.ds(base, ts), sl].astype(F32)
            out, carry = _rglru_lanes(x, tail, y, carry, cw_ref[:, sl], cb_ref[:, sl], wg_ref[n], br_ref[:, sl],
                                      bi_ref[:, sl], lam_ref[:, sl], row8, row_in_tile)
            rec_ref[:, sl] = out.astype(rec_ref.dtype)
            carry_s[0:1, sl] = carry

    _run_casts(cast_in, cast_out)


def _in_proj(xg, r, w, colscale, seq, conv_w, conv_b, w_gates, b_r, b_i, lru_lambda, casts):
    t, k = xg.shape
    n = w.shape[1]
    bm, bn = TILES["in_proj"]
    ts = TILES["rglru_rows"]
    grid = (t // bm, n // bn)
    n_sub = bm // ts
    j_rg = OFF_LY // bn + 1
    assert bn == LRU_W and OFF_LX % bn == 0 and seq % bm == 0 and bm % ts == 0 and j_rg + n_sub <= grid[1]
    assert (bm // (LRU_W // LRU_BLOCK_W)) % BF16_ROWS == 0
    c_in, c_out, c_shapes, c_args = _cast_plan(casts, grid)
    vec = lambda a: a.reshape(1, LRU_W)
    const = lambda i, j: (0, 0)
    return pl.pallas_call(
        functools.partial(_in_proj_kernel, n_cast=len(casts), ts=ts, blocks_per_seq=seq // bm),
        grid=grid,
        in_specs=[pl.BlockSpec((bm, k), lambda i, j: (i, 0)),
                  pl.BlockSpec((bm, LANES), lambda i, j: (i, 0)),
                  pl.BlockSpec((k, bn), lambda i, j: (0, j)),
                  pl.BlockSpec((1, bn), lambda i, j: (0, j)),
                  pl.BlockSpec((CONV_W, LRU_W), const),
                  pl.BlockSpec((1, LRU_W), const),
                  pl.BlockSpec((LRU_W // LRU_BLOCK_W, LRU_BLOCK_W, 2 * LRU_BLOCK_W), lambda i, j: (0, 0, 0)),
                  pl.BlockSpec((1, LRU_W), const),
                  pl.BlockSpec((1, LRU_W), const),
                  pl.BlockSpec((1, LRU_W), const)] + c_in,
        out_specs=[pl.BlockSpec((bm, bn), lambda i, j: (i, j)),
                   pl.BlockSpec((ts, LRU_W), lambda i, j: (i * n_sub + jnp.clip(j - j_rg, 0, n_sub - 1), 0))] + c_out,
        out_shape=[jax.ShapeDtypeStruct((t, n), BF16), jax.ShapeDtypeStruct((t, LRU_W), BF16)] + c_shapes,
        scratch_shapes=[pltpu.VMEM((bm, LRU_W), BF16), pltpu.VMEM((bm, LRU_W), BF16),
                        pltpu.VMEM((BF16_ROWS, LRU_W), BF16), pltpu.VMEM((SUBLANES, LRU_W), F32)],
        compiler_params=_params("arbitrary", "arbitrary"),
        name="in_proj",
    )(xg, r, w, colscale, conv_w, vec(conv_b), w_gates, vec(b_r), vec(b_i), vec(lru_lambda), *c_args)


def kernel(x, norm1_g, w_in, gate_b, lam_qk, subln_g, w_attn_proj, conv_w, conv_b, w_rgate, b_rgate,
           w_igate, b_igate, lru_lambda, w_rec_proj, w_out, norm2_g, w_up, w_down, final_g):
    bsz, seq, d = x.shape
    t = bsz * seq
    x = x.reshape(t, d)
    q_scale = jnp.ones((1, C_IN), F32).at[:, OFF_Q:OFF_Q + ATTN_W].set(HEAD_DIM ** -0.5 * LOG2_E)
    wb_in, wb_att, wb_rec = (w[0].astype(BF16) for w in (w_in, w_attn_proj, w_rec_proj))
    xg, r = _prep(x, norm1_g[0])
    for l in range(DEPTH):
        first, last = l == 0, l == DEPTH - 1
        nxt = lambda *ws: [] if last else [(w, l + 1) for w in ws]
        now = lambda *ws: [(w, 0) for w in ws] if first else []
        lam_init = 0.8 - 0.6 * math.exp(-0.3 * l)
        w_gates = jnp.concatenate([w_rgate[l], w_igate[l]], axis=-1).astype(BF16)
        z, rec, *nb_in = _in_proj(xg, r, wb_in, q_scale, seq, conv_w[l], conv_b[l], w_gates, b_rgate[l], b_igate[l],
                                  lru_lambda[l], nxt(w_in))
        att = _attention(z.reshape(bsz, seq, C_IN), lam_qk[l], subln_g[l], lam_init)
        m, *conv = _merge(att.reshape(t, ATTN_W), rec, wb_att, wb_rec, z, gate_b[l],
                          now(w_out) + nxt(w_attn_proj, w_rec_proj))
        if first:
            wb_out = conv.pop(0)
        nb_proj = conv
        x, xg, r, *conv = _residual_matmul(m, wb_out, x, norm2_g[l], now(w_up) + nxt(w_out), name="out_proj")
        if first:
            wb_up = conv.pop(0)
        nb_out = conv
        u, *conv = _up_proj(xg, r, wb_up, now(w_down) + nxt(w_up, w_down))
        if first:
            wb_down = conv.pop(0)
        nb_up = conv
        if last:
            x, r = _residual_matmul(u, wb_down, x, None, [], name="down_proj")
        else:
            x, xg, r = _residual_matmul(u, wb_down, x, norm1_g[l + 1], [], name="down_proj")
            (wb_in,), (wb_att, wb_rec), (wb_out,), (wb_up, wb_down) = nb_in, nb_proj, nb_out, nb_up
    return _final_norm(x, r, final_g).reshape(bsz, seq, d)
```

```python
import functools
import math

import numpy as np
import jax
import jax.numpy as jnp
from jax import lax
from jax.experimental import pallas as pl
from jax.experimental.pallas import tpu as pltpu

D_MODEL = 4096
DEPTH = 4
N_HEADS = 8
HEAD_DIM = 128
HEAD_W = 2 * HEAD_DIM
ATTN_W = N_HEADS * HEAD_W
LRU_W = 1024
LRU_BLOCKS = 8
LRU_BLOCK_W = LRU_W // LRU_BLOCKS
CONV_W = 4
LRU_C = 8.0
D_FF = 4 * D_MODEL
EPS = 1e-6
C_IN = 3 * ATTN_W + 2 * LRU_W + 2 * D_MODEL
OFF_Q, OFF_K, OFF_V = 0, ATTN_W, 2 * ATTN_W
OFF_LX = 3 * ATTN_W
OFF_LY = OFF_LX + LRU_W
OFF_GA = OFF_LY + LRU_W
OFF_GR = OFF_GA + D_MODEL

LANES = 128
SUBLANES = 8
BF16_ROWS = 16
VMEM_LIMIT = 64 * 1024 * 1024
LOG2_E = math.log2(math.e)

TILES = {
    "norm_rows": 256,
    "in_proj": (1024, 1024),
    "up_proj": (1024, 1024),
    "merge": (1024, 1024),
    "out_proj": (1024, 512, D_MODEL),
    "down_proj": (1024, 1024, 4096),
    "attn_q_rows": 256,
    "rglru_rows": 256,
}

BF16 = jnp.bfloat16
F32 = jnp.float32


def _params(*sem):
    return pltpu.CompilerParams(dimension_semantics=sem, vmem_limit_bytes=VMEM_LIMIT)


def _row_scale(r_ref):
    return r_ref[:, 0:1]


def _cast_plan(casts, grid):
    steps = math.prod(grid)

    def linear(*g):
        idx = g[0]
        for size, gi in zip(grid[1:], g[1:]):
            idx = idx * size + gi
        return idx

    in_specs, out_specs, out_shapes, args = [], [], [], []
    for w, layer in casts:
        _, k, n = w.shape
        rows = k // steps
        assert rows * steps == k and rows % BF16_ROWS == 0, (w.shape, grid)
        in_specs.append(pl.BlockSpec((None, rows, n), lambda *g, layer=layer: (layer, linear(*g), 0)))
        out_specs.append(pl.BlockSpec((rows, n), lambda *g: (linear(*g), 0)))
        out_shapes.append(jax.ShapeDtypeStruct((k, n), BF16))
        args.append(w)
    return in_specs, out_specs, out_shapes, args


def _run_casts(cast_in, cast_out):
    for src, dst in zip(cast_in, cast_out):
        dst[...] = src[...].astype(dst.dtype)


def _prep_kernel(x_ref, g_ref, xg_ref, r_ref):
    x = x_ref[...]
    r = lax.rsqrt(jnp.mean(x * x, axis=-1, keepdims=True) + EPS)
    xg_ref[...] = (x * g_ref[...]).astype(xg_ref.dtype)
    r_ref[...] = jnp.broadcast_to(r, r_ref.shape)


def _prep(x, g):
    t, d = x.shape
    bm = TILES["norm_rows"]
    return pl.pallas_call(
        _prep_kernel,
        grid=(t // bm,),
        in_specs=[pl.BlockSpec((bm, d), lambda i: (i, 0)),
                  pl.BlockSpec((1, d), lambda i: (0, 0))],
        out_specs=[pl.BlockSpec((bm, d), lambda i: (i, 0)),
                   pl.BlockSpec((bm, LANES), lambda i: (i, 0))],
        out_shape=[jax.ShapeDtypeStruct((t, d), BF16),
                   jax.ShapeDtypeStruct((t, LANES), F32)],
        compiler_params=_params("parallel"),
        name="norm_prep",
    )(x, g.reshape(1, d))


def _final_norm_kernel(x_ref, r_ref, g_ref, o_ref):
    o_ref[...] = (x_ref[...] * _row_scale(r_ref)) * g_ref[...]


def _final_norm(x, r, g):
    t, d = x.shape
    bm = TILES["norm_rows"]
    return pl.pallas_call(
        _final_norm_kernel,
        grid=(t // bm,),
        in_specs=[pl.BlockSpec((bm, d), lambda i: (i, 0)),
                  pl.BlockSpec((bm, LANES), lambda i: (i, 0)),
                  pl.BlockSpec((1, d), lambda i: (0, 0))],
        out_specs=pl.BlockSpec((bm, d), lambda i: (i, 0)),
        out_shape=jax.ShapeDtypeStruct((t, d), F32),
        compiler_params=_params("parallel"),
        name="final_norm",
    )(x, r, g.reshape(1, d))


def _up_proj_kernel(*refs, n_cast):
    a_ref, r_ref, b_ref = refs[:3]
    cast_in = refs[3:3 + n_cast]
    o_ref = refs[3 + n_cast]
    cast_out = refs[4 + n_cast:]
    acc = jnp.dot(a_ref[...], b_ref[...], preferred_element_type=F32)
    o_ref[...] = jnp.square(jnp.maximum(acc * _row_scale(r_ref), 0.0)).astype(o_ref.dtype)
    _run_casts(cast_in, cast_out)


def _up_proj(xg, r, w, casts):
    t, k = xg.shape
    n = w.shape[1]
    bm, bn = TILES["up_proj"]
    grid = (t // bm, n // bn)
    c_in, c_out, c_shapes, c_args = _cast_plan(casts, grid)
    return pl.pallas_call(
        functools.partial(_up_proj_kernel, n_cast=len(casts)),
        grid=grid,
        in_specs=[pl.BlockSpec((bm, k), lambda i, j: (i, 0)),
                  pl.BlockSpec((bm, LANES), lambda i, j: (i, 0)),
                  pl.BlockSpec((k, bn), lambda i, j: (0, j))] + c_in,
        out_specs=[pl.BlockSpec((bm, bn), lambda i, j: (i, j))] + c_out,
        out_shape=[jax.ShapeDtypeStruct((t, n), BF16)] + c_shapes,
        compiler_params=_params("parallel", "parallel"),
        name="up_proj",
    )(xg, r, w, *c_args)


def _merge_kernel(*refs, n_cast):
    att_ref, rec_ref, wa_ref, wr_ref, ga_ref, gr_ref, gb_ref = refs[:7]
    cast_in = refs[7:7 + n_cast]
    o_ref = refs[7 + n_cast]
    cast_out = refs[8 + n_cast:]
    y_att = jnp.dot(att_ref[...], wa_ref[...], preferred_element_type=F32)
    y_rec = jnp.dot(rec_ref[...], wr_ref[...], preferred_element_type=F32)
    g_att = jax.nn.sigmoid(ga_ref[...].astype(F32) + gb_ref[0:1, :])
    g_rec = jax.nn.sigmoid(gr_ref[...].astype(F32) + gb_ref[1:2, :])
    o_ref[...] = (g_att * y_att + g_rec * y_rec).astype(o_ref.dtype)
    _run_casts(cast_in, cast_out)


def _merge(att, rec, w_att, w_rec, z, gate_b, casts):
    t = att.shape[0]
    n = w_att.shape[1]
    bm, bn = TILES["merge"]
    grid = (t // bm, n // bn)
    ga_blk, gr_blk = OFF_GA // bn, OFF_GR // bn
    c_in, c_out, c_shapes, c_args = _cast_plan(casts, grid)
    return pl.pallas_call(
        functools.partial(_merge_kernel, n_cast=len(casts)),
        grid=grid,
        in_specs=[pl.BlockSpec((bm, ATTN_W), lambda i, j: (i, 0)),
                  pl.BlockSpec((bm, LRU_W), lambda i, j: (i, 0)),
                  pl.BlockSpec((ATTN_W, bn), lambda i, j: (0, j)),
                  pl.BlockSpec((LRU_W, bn), lambda i, j: (0, j)),
                  pl.BlockSpec((bm, bn), lambda i, j: (i, j + ga_blk)),
                  pl.BlockSpec((bm, bn), lambda i, j: (i, j + gr_blk)),
                  pl.BlockSpec((2, bn), lambda i, j: (0, j))] + c_in,
        out_specs=[pl.BlockSpec((bm, bn), lambda i, j: (i, j))] + c_out,
        out_shape=[jax.ShapeDtypeStruct((t, n), BF16)] + c_shapes,
        compiler_params=_params("parallel", "parallel"),
        name="merge",
    )(att, rec, w_att, w_rec, z, z, gate_b, *c_args)


def _residual_matmul_kernel(*refs, nj, nk, n_cast, emit_xg):
    a_ref, b_ref, x_ref = refs[:3]
    n_in = 3 + emit_xg
    g_ref = refs[3] if emit_xg else None
    cast_in = refs[n_in:n_in + n_cast]
    outs = refs[n_in + n_cast:len(refs) - 1]
    ss_ref = refs[-1]
    o_ref = outs[0]
    xg_ref = outs[1] if emit_xg else None
    r_ref = outs[1 + emit_xg]
    cast_out = outs[2 + emit_xg:]
    j, k = pl.program_id(1), pl.program_id(2)

    def product():
        return jnp.dot(a_ref[...], b_ref[...], preferred_element_type=F32)

    if nk > 1:
        @pl.when(k == 0)
        def _():
            o_ref[...] = x_ref[...] + product()

        @pl.when(jnp.logical_and(k > 0, k < nk - 1))
        def _():
            o_ref[...] += product()

    @pl.when(k == nk - 1)
    def _():
        xn = (o_ref[...] if nk > 1 else x_ref[...]) + product()
        o_ref[...] = xn
        if emit_xg:
            xg_ref[...] = (xn * g_ref[...]).astype(xg_ref.dtype)
        sq = xn * xn
        part = sq[:, 0:LANES]
        for c in range(1, sq.shape[1] // LANES):
            part = part + sq[:, c * LANES:(c + 1) * LANES]

        @pl.when(j == 0)
        def _():
            ss_ref[...] = part

        @pl.when(j > 0)
        def _():
            ss_ref[...] += part

        @pl.when(j == nj - 1)
        def _():
            mean_sq = jnp.sum(ss_ref[...], axis=-1, keepdims=True) * (1.0 / (nj * sq.shape[1]))
            r_ref[...] = jnp.broadcast_to(lax.rsqrt(mean_sq + EPS), r_ref.shape)

    _run_casts(cast_in, cast_out)


def _residual_matmul(a, w, x, g_next, casts, name):
    t, kdim = a.shape
    n = w.shape[1]
    bm, bn, bk = TILES[name]
    nj, nk = n // bn, kdim // bk
    grid = (t // bm, nj, nk)
    emit_xg = g_next is not None
    c_in, c_out, c_shapes, c_args = _cast_plan(casts, grid)
    tile = pl.BlockSpec((bm, bn), lambda i, j, k: (i, j))
    out_specs = [tile] + ([tile] if emit_xg else []) + [pl.BlockSpec((bm, LANES), lambda i, j, k: (i, 0))]
    out_shape = ([jax.ShapeDtypeStruct((t, n), F32)] + ([jax.ShapeDtypeStruct((t, n), BF16)] if emit_xg else [])
                 + [jax.ShapeDtypeStruct((t, LANES), F32)])
    return pl.pallas_call(
        functools.partial(_residual_matmul_kernel, nj=nj, nk=nk, n_cast=len(casts), emit_xg=emit_xg),
        grid=grid,
        in_specs=[pl.BlockSpec((bm, bk), lambda i, j, k: (i, k)),
                  pl.BlockSpec((bk, bn), lambda i, j, k: (k, j)),
                  tile]
                 + ([pl.BlockSpec((1, bn), lambda i, j, k: (0, j))] if emit_xg else []) + c_in,
        out_specs=out_specs + c_out,
        out_shape=out_shape + c_shapes,
        scratch_shapes=[pltpu.VMEM((bm, LANES), F32)],
        compiler_params=_params("parallel", "arbitrary", "arbitrary"),
        name=name,
    )(a, w, x, *([g_next.reshape(1, n)] if emit_xg else []), *c_args)


def _attn_kernel(q_ref, k_ref, v_ref, lq_ref, sg_ref, o_ref, *, lam_init, tq):
    seq = q_ref.shape[1]
    lq = lq_ref[...]
    lam = (jnp.exp(jnp.sum(lq[0:1] * lq[1:2], axis=-1, keepdims=True))
           - jnp.exp(jnp.sum(lq[2:3] * lq[3:4], axis=-1, keepdims=True)) + lam_init)
    contract_last = (((1,), (1,)), ((), ()))
    row = lax.broadcasted_iota(jnp.int32, (tq, tq), 0)
    col = lax.broadcasted_iota(jnp.int32, (tq, tq), 1)
    visible = row >= col
    def scores(i, c):
        off = i * tq
        comp = slice(c * HEAD_DIM, (c + 1) * HEAD_DIM)
        q = q_ref[0, off:off + tq, comp]
        s_d = lax.dot_general(q, k_ref[0, off:off + tq, comp], contract_last, preferred_element_type=F32)
        s_d = jnp.where(visible, s_d, -jnp.inf)
        s_p = None
        if i > 0:
            s_p = lax.dot_general(q, k_ref[0, 0:off, comp], contract_last, preferred_element_type=F32)
        return s_d, s_p

    def numerators(i, c, sc):
        s_d, s_p = sc
        m = jnp.max(s_d, axis=-1, keepdims=True)
        e_p = None
        if i > 0:
            m = jnp.maximum(m, jnp.max(s_p, axis=-1, keepdims=True))
            e_p = jnp.exp2(s_p - m)
        e_d = jnp.exp2(s_d - m)
        denom = jnp.sum(e_d, axis=-1, keepdims=True)
        if i > 0:
            denom = denom + jnp.sum(e_p, axis=-1, keepdims=True)
        return e_d, e_p, (1.0 if c == 0 else lam) / denom

    def weights(i, n0, n1):
        w_d = (n0[0] * n0[2] - n1[0] * n1[2]).astype(BF16)
        w_p = (n0[1] * n0[2] - n1[1] * n1[2]).astype(BF16) if i > 0 else None
        return w_d, w_p

    def finish(i, w):
        off = i * tq
        w_d, w_p = w
        o = jnp.dot(w_d, v_ref[0, off:off + tq, :], preferred_element_type=F32)
        if i > 0:
            o = o + jnp.dot(w_p, v_ref[0, 0:off, :], preferred_element_type=F32)
        o = o * lax.rsqrt(jnp.mean(o * o, axis=-1, keepdims=True) + EPS)
        o = (o * sg_ref[...]) * (1.0 - lam_init)
        o_ref[0, off:off + tq, :] = o.astype(o_ref.dtype)

    n_blocks = seq // tq
    sc = {(0, c): scores(0, c) for c in range(2)}
    w = {}
    for i in range(n_blocks + 1):
        num = []
        for c in range(2):
            if i + 1 < n_blocks:
                sc[i + 1, c] = scores(i + 1, c)
            if i < n_blocks:
                num.append(numerators(i, c, sc.pop((i, c))))
        if i >= 1:
            finish(i - 1, w.pop(i - 1))
        if i < n_blocks:
            w[i] = weights(i, *num)


def _attention(z3, lam_qk, subln_g, lam_init):
    b, s, _ = z3.shape
    tq = TILES["attn_q_rows"]
    kb, vb = OFF_K // HEAD_W, OFF_V // HEAD_W
    return pl.pallas_call(
        functools.partial(_attn_kernel, lam_init=lam_init, tq=tq),
        grid=(b, N_HEADS),
        in_specs=[pl.BlockSpec((1, s, HEAD_W), lambda bi, h: (bi, 0, h)),
                  pl.BlockSpec((1, s, HEAD_W), lambda bi, h: (bi, 0, h + kb)),
                  pl.BlockSpec((1, s, HEAD_W), lambda bi, h: (bi, 0, h + vb)),
                  pl.BlockSpec((4, HEAD_DIM), lambda bi, h: (0, 0)),
                  pl.BlockSpec((1, HEAD_W), lambda bi, h: (0, 0))],
        out_specs=pl.BlockSpec((1, s, HEAD_W), lambda bi, h: (bi, 0, h)),
        out_shape=jax.ShapeDtypeStruct((b, s, ATTN_W), BF16),
        compiler_params=_params("parallel", "parallel"),
        name="diff_attention",
    )(z3, z3, z3, lam_qk, subln_g.reshape(1, HEAD_W))


def _softplus(x):
    return jnp.maximum(x, 0.0) + jnp.log1p(jnp.exp(-jnp.abs(x)))


def _gelu_tanh(x):
    sqrt_2_over_pi = np.sqrt(2 / np.pi).astype(np.float32)
    cdf = 0.5 * (1.0 + jnp.tanh(sqrt_2_over_pi * (x + 0.044715 * (x * x * x))))
    return x * cdf


def _rglru_lanes(x, tail, y, carry, cw, cb, wg, b_r, b_i, lam, row8, row_in_tile):
    ts = x.shape[0]
    n_tiles = ts // SUBLANES
    xr = cb + cw[CONV_W - 1:CONV_W] * x
    for d in range(1, CONV_W):
        rolled = pltpu.roll(x, d, 0)
        head = jnp.where(row8 < d, pltpu.roll(tail, d, 0), rolled[:SUBLANES, :])
        shifted = jnp.concatenate([head, rolled[SUBLANES:, :]], axis=0)
        xr = xr + cw[CONV_W - 1 - d:CONV_W - d] * shifted
    gates = jnp.dot(xr.astype(BF16), wg, preferred_element_type=F32)
    r = jax.nn.sigmoid(gates[:, :LRU_BLOCK_W] + b_r)
    ig = jax.nn.sigmoid(gates[:, LRU_BLOCK_W:] + b_i)
    log_a = -LRU_C * r * _softplus(-lam)
    a = jnp.exp(log_a)
    th = jnp.tanh(log_a)
    bt = jnp.sqrt(-2.0 * th / (1.0 - th)) * (ig * xr)
    a3 = a.reshape(n_tiles, SUBLANES, LRU_BLOCK_W)
    b3 = bt.reshape(n_tiles, SUBLANES, LRU_BLOCK_W)
    d = 1
    while d < SUBLANES:
        keep = row_in_tile >= d
        a_sh = jnp.where(keep, pltpu.roll(a3, d, 1), 1.0)
        b_sh = jnp.where(keep, pltpu.roll(b3, d, 1), 0.0)
        b3 = a3 * b_sh + b3
        a3 = a3 * a_sh
        d *= 2
    tiles = []
    for v in range(n_tiles):
        h_v = a3[v] * carry + b3[v]
        tiles.append(h_v)
        carry = h_v[SUBLANES - 1:SUBLANES, :]
    h = jnp.concatenate(tiles, axis=0)
    return h * _gelu_tanh(y), carry


def _in_proj_kernel(*refs, n_cast, ts, blocks_per_seq):
    xg_ref, r_ref, w_ref, cs_ref, cw_ref, cb_ref, wg_ref, br_ref, bi_ref, lam_ref = refs[:10]
    cast_in = refs[10:10 + n_cast]
    z_ref, rec_ref = refs[10 + n_cast:12 + n_cast]
    cast_out = refs[12 + n_cast:len(refs) - 4]
    lx_s, ly_s, tail_s, carry_s = refs[-4:]
    i, j = pl.program_id(0), pl.program_id(1)
    bm, bn = z_ref.shape
    n_sub = bm // ts
    n_lane_blocks = LRU_W // LRU_BLOCK_W
    chunk = bm // n_lane_blocks
    j_lx, j_ly = OFF_LX // bn, OFF_LY // bn
    j_rg = j_ly + 1

    def project(rows):
        acc = jnp.dot(xg_ref[rows, :], w_ref[...], preferred_element_type=F32)
        return ((acc * r_ref[rows, 0:1]) * cs_ref[...]).astype(z_ref.dtype)

    @pl.when(jnp.logical_and(i == 0, j == 0))
    def _():
        lx_s[bm - BF16_ROWS:bm, :] = jnp.zeros((BF16_ROWS, LRU_W), lx_s.dtype)

    runs_rglru = jnp.logical_and(j >= j_rg, j < j_rg + n_sub)

    @pl.when(jnp.logical_not(runs_rglru))
    def _():
        z = project(slice(None))
        z_ref[...] = z

        @pl.when(j == j_lx)
        def _():
            tail_s[...] = lx_s[bm - BF16_ROWS:bm, :]
            lx_s[...] = z

        @pl.when(j == j_ly)
        def _():
            ly_s[...] = z

    @pl.when(runs_rglru)
    def _():
        sub = j - j_rg
        base = pl.multiple_of(sub * ts, ts)
        before = pl.multiple_of(jnp.maximum(base - BF16_ROWS, 0), BF16_ROWS)
        seq_start = jnp.logical_and(i % blocks_per_seq == 0, sub == 0)
        row8 = lax.broadcasted_iota(jnp.int32, (SUBLANES, LRU_BLOCK_W), 0)
        row_in_tile = lax.broadcasted_iota(jnp.int32, (ts // SUBLANES, SUBLANES, LRU_BLOCK_W), 1)
        for n in range(n_lane_blocks):
            rows = slice(n * chunk, (n + 1) * chunk)
            z_ref[rows, :] = project(rows)
            sl = slice(n * LRU_BLOCK_W, (n + 1) * LRU_BLOCK_W)
            x = lx_s[pl.ds(base, ts), sl].astype(F32)
            prev = jnp.where(sub > 0, lx_s[pl.ds(before, BF16_ROWS), sl], tail_s[:, sl]).astype(F32)
            tail = jnp.where(seq_start, 0.0, prev[BF16_ROWS - SUBLANES:, :])
            carry = jnp.where(seq_start, 0.0, carry_s[0:1, sl])
            y = ly_s[pl.ds(base, ts), sl].astype(F32)
            out, carry = _rglru_lanes(x, tail, y, carry, cw_ref[:, sl], cb_ref[:, sl], wg_ref[n], br_ref[:, sl],
                                      bi_ref[:, sl], lam_ref[:, sl], row8, row_in_tile)
            rec_ref[:, sl] = out.astype(rec_ref.dtype)
            carry_s[0:1, sl] = carry

    _run_casts(cast_in, cast_out)


def _in_proj(xg, r, w, colscale, seq, conv_w, conv_b, w_gates, b_r, b_i, lru_lambda, casts):
    t, k = xg.shape
    n = w.shape[1]
    bm, bn = TILES["in_proj"]
    ts = TILES["rglru_rows"]
    grid = (t // bm, n // bn)
    n_sub = bm // ts
    j_rg = OFF_LY // bn + 1
    assert bn == LRU_W and OFF_LX % bn == 0 and seq % bm == 0 and bm % ts == 0 and j_rg + n_sub <= grid[1]
    assert (bm // (LRU_W // LRU_BLOCK_W)) % BF16_ROWS == 0
    c_in, c_out, c_shapes, c_args = _cast_plan(casts, grid)
    vec = lambda a: a.reshape(1, LRU_W)
    const = lambda i, j: (0, 0)
    return pl.pallas_call(
        functools.partial(_in_proj_kernel, n_cast=len(casts), ts=ts, blocks_per_seq=seq // bm),
        grid=grid,
        in_specs=[pl.BlockSpec((bm, k), lambda i, j: (i, 0)),
                  pl.BlockSpec((bm, LANES), lambda i, j: (i, 0)),
                  pl.BlockSpec((k, bn), lambda i, j: (0, j)),
                  pl.BlockSpec((1, bn), lambda i, j: (0, j)),
                  pl.BlockSpec((CONV_W, LRU_W), const),
                  pl.BlockSpec((1, LRU_W), const),
                  pl.BlockSpec((LRU_W // LRU_BLOCK_W, LRU_BLOCK_W, 2 * LRU_BLOCK_W), lambda i, j: (0, 0, 0)),
                  pl.BlockSpec((1, LRU_W), const),
                  pl.BlockSpec((1, LRU_W), const),
                  pl.BlockSpec((1, LRU_W), const)] + c_in,
        out_specs=[pl.BlockSpec((bm, bn), lambda i, j: (i, j)),
                   pl.BlockSpec((ts, LRU_W), lambda i, j: (i * n_sub + jnp.clip(j - j_rg, 0, n_sub - 1), 0))] + c_out,
        out_shape=[jax.ShapeDtypeStruct((t, n), BF16), jax.ShapeDtypeStruct((t, LRU_W), BF16)] + c_shapes,
        scratch_shapes=[pltpu.VMEM((bm, LRU_W), BF16), pltpu.VMEM((bm, LRU_W), BF16),
                        pltpu.VMEM((BF16_ROWS, LRU_W), BF16), pltpu.VMEM((SUBLANES, LRU_W), F32)],
        compiler_params=_params("arbitrary", "arbitrary"),
        name="in_proj",
    )(xg, r, w, colscale, conv_w, vec(conv_b), w_gates, vec(b_r), vec(b_i), vec(lru_lambda), *c_args)


def kernel(x, norm1_g, w_in, gate_b, lam_qk, subln_g, w_attn_proj, conv_w, conv_b, w_rgate, b_rgate,
           w_igate, b_igate, lru_lambda, w_rec_proj, w_out, norm2_g, w_up, w_down, final_g):
    bsz, seq, d = x.shape
    t = bsz * seq
    x = x.reshape(t, d)
    q_scale = jnp.ones((1, C_IN), F32).at[:, OFF_Q:OFF_Q + ATTN_W].set(HEAD_DIM ** -0.5 * LOG2_E)
    wb_in, wb_att, wb_rec = (w[0].astype(BF16) for w in (w_in, w_attn_proj, w_rec_proj))
    xg, r = _prep(x, norm1_g[0])
    for l in range(DEPTH):
        first, last = l == 0, l == DEPTH - 1
        nxt = lambda *ws: [] if last else [(w, l + 1) for w in ws]
        now = lambda *ws: [(w, 0) for w in ws] if first else []
        lam_init = 0.8 - 0.6 * math.exp(-0.3 * l)
        w_gates = jnp.concatenate([w_rgate[l], w_igate[l]], axis=-1).astype(BF16)
        z, rec, *nb_in = _in_proj(xg, r, wb_in, q_scale, seq, conv_w[l], conv_b[l], w_gates, b_rgate[l], b_igate[l],
                                  lru_lambda[l], nxt(w_in))
        att = _attention(z.reshape(bsz, seq, C_IN), lam_qk[l], subln_g[l], lam_init)
        m, *conv = _merge(att.reshape(t, ATTN_W), rec, wb_att, wb_rec, z, gate_b[l],
                          now(w_out) + nxt(w_attn_proj, w_rec_proj))
        if first:
            wb_out = conv.pop(0)
        nb_proj = conv
        x, xg, r, *conv = _residual_matmul(m, wb_out, x, norm2_g[l], now(w_up) + nxt(w_out), name="out_proj")
        if first:
            wb_up = conv.pop(0)
        nb_out = conv
        u, *conv = _up_proj(xg, r, wb_up, now(w_down) + nxt(w_up, w_down))
        if first:
            wb_down = conv.pop(0)
        nb_up = conv
        if last:
            x, r = _residual_matmul(u, wb_down, x, None, [], name="down_proj")
        else:
            x, xg, r = _residual_matmul(u, wb_down, x, norm1_g[l + 1], [], name="down_proj")
            (wb_in,), (wb_att, wb_rec), (wb_out,), (wb_up, wb_down) = nb_in, nb_proj, nb_out, nb_up
    return _final_norm(x, r, final_g).reshape(bsz, seq, d)
```

```python
import functools
import math

import numpy as np
import jax
import jax.numpy as jnp
from jax import lax
from jax.experimental import pallas as pl
from jax.experimental.pallas import tpu as pltpu

D_MODEL = 4096
DEPTH = 4
N_HEADS = 8
HEAD_DIM = 128
HEAD_W = 2 * HEAD_DIM
ATTN_W = N_HEADS * HEAD_W
LRU_W = 1024
LRU_BLOCKS = 8
LRU_BLOCK_W = LRU_W // LRU_BLOCKS
CONV_W = 4
LRU_C = 8.0
D_FF = 4 * D_MODEL
EPS = 1e-6
C_IN = 3 * ATTN_W + 2 * LRU_W + 2 * D_MODEL
OFF_Q, OFF_K, OFF_V = 0, ATTN_W, 2 * ATTN_W
OFF_LX = 3 * ATTN_W
OFF_LY = OFF_LX + LRU_W
OFF_GA = OFF_LY + LRU_W
OFF_GR = OFF_GA + D_MODEL

LANES = 128
SUBLANES = 8
BF16_ROWS = 16
VMEM_LIMIT = 64 * 1024 * 1024
LOG2_E = math.log2(math.e)

TILES = {
    "norm_rows": 256,
    "in_proj": (1024, 1024),
    "up_proj": (1024, 1024),
    "merge": (1024, 1024),
    "out_proj": (1024, 1024, D_MODEL),
    "down_proj": (1024, 1024, 4096),
    "attn_q_rows": 256,
    "rglru_rows": 256,
}

BF16 = jnp.bfloat16
F32 = jnp.float32


def _params(*sem):
    return pltpu.CompilerParams(dimension_semantics=sem, vmem_limit_bytes=VMEM_LIMIT)


def _row_scale(r_ref):
    return r_ref[:, 0:1]


def _cast_plan(casts, grid):
    steps = math.prod(grid)

    def linear(*g):
        idx = g[0]
        for size, gi in zip(grid[1:], g[1:]):
            idx = idx * size + gi
        return idx

    in_specs, out_specs, out_shapes, args = [], [], [], []
    for w, layer in casts:
        _, k, n = w.shape
        rows = k // steps
        assert rows * steps == k and rows % BF16_ROWS == 0, (w.shape, grid)
        in_specs.append(pl.BlockSpec((None, rows, n), lambda *g, layer=layer: (layer, linear(*g), 0)))
        out_specs.append(pl.BlockSpec((rows, n), lambda *g: (linear(*g), 0)))
        out_shapes.append(jax.ShapeDtypeStruct((k, n), BF16))
        args.append(w)
    return in_specs, out_specs, out_shapes, args


def _run_casts(cast_in, cast_out):
    for src, dst in zip(cast_in, cast_out):
        dst[...] = src[...].astype(dst.dtype)


def _prep_kernel(x_ref, g_ref, xg_ref, r_ref):
    x = x_ref[...]
    r = lax.rsqrt(jnp.mean(x * x, axis=-1, keepdims=True) + EPS)
    xg_ref[...] = (x * g_ref[...]).astype(xg_ref.dtype)
    r_ref[...] = jnp.broadcast_to(r, r_ref.shape)


def _prep(x, g):
    t, d = x.shape
    bm = TILES["norm_rows"]
    return pl.pallas_call(
        _prep_kernel,
        grid=(t // bm,),
        in_specs=[pl.BlockSpec((bm, d), lambda i: (i, 0)),
                  pl.BlockSpec((1, d), lambda i: (0, 0))],
        out_specs=[pl.BlockSpec((bm, d), lambda i: (i, 0)),
                   pl.BlockSpec((bm, LANES), lambda i: (i, 0))],
        out_shape=[jax.ShapeDtypeStruct((t, d), BF16),
                   jax.ShapeDtypeStruct((t, LANES), F32)],
        compiler_params=_params("parallel"),
        name="norm_prep",
    )(x, g.reshape(1, d))


def _final_norm_kernel(x_ref, r_ref, g_ref, o_ref):
    o_ref[...] = (x_ref[...] * _row_scale(r_ref)) * g_ref[...]


def _final_norm(x, r, g):
    t, d = x.shape
    bm = TILES["norm_rows"]
    return pl.pallas_call(
        _final_norm_kernel,
        grid=(t // bm,),
        in_specs=[pl.BlockSpec((bm, d), lambda i: (i, 0)),
                  pl.BlockSpec((bm, LANES), lambda i: (i, 0)),
                  pl.BlockSpec((1, d), lambda i: (0, 0))],
        out_specs=pl.BlockSpec((bm, d), lambda i: (i, 0)),
        out_shape=jax.ShapeDtypeStruct((t, d), F32),
        compiler_params=_params("parallel"),
        name="final_norm",
    )(x, r, g.reshape(1, d))


def _up_proj_kernel(*refs, n_cast):
    a_ref, r_ref, b_ref = refs[:3]
    cast_in = refs[3:3 + n_cast]
    o_ref = refs[3 + n_cast]
    cast_out = refs[4 + n_cast:]
    acc = jnp.dot(a_ref[...], b_ref[...], preferred_element_type=F32)
    o_ref[...] = jnp.square(jnp.maximum(acc * _row_scale(r_ref), 0.0)).astype(o_ref.dtype)
    _run_casts(cast_in, cast_out)


def _up_proj(xg, r, w, casts):
    t, k = xg.shape
    n = w.shape[1]
    bm, bn = TILES["up_proj"]
    grid = (t // bm, n // bn)
    c_in, c_out, c_shapes, c_args = _cast_plan(casts, grid)
    return pl.pallas_call(
        functools.partial(_up_proj_kernel, n_cast=len(casts)),
        grid=grid,
        in_specs=[pl.BlockSpec((bm, k), lambda i, j: (i, 0)),
                  pl.BlockSpec((bm, LANES), lambda i, j: (i, 0)),
                  pl.BlockSpec((k, bn), lambda i, j: (0, j))] + c_in,
        out_specs=[pl.BlockSpec((bm, bn), lambda i, j: (i, j))] + c_out,
        out_shape=[jax.ShapeDtypeStruct((t, n), BF16)] + c_shapes,
        compiler_params=_params("parallel", "parallel"),
        name="up_proj",
    )(xg, r, w, *c_args)


def _merge_kernel(*refs, n_cast):
    att_ref, rec_ref, wa_ref, wr_ref, ga_ref, gr_ref, gb_ref = refs[:7]
    cast_in = refs[7:7 + n_cast]
    o_ref = refs[7 + n_cast]
    cast_out = refs[8 + n_cast:]
    y_att = jnp.dot(att_ref[...], wa_ref[...], preferred_element_type=F32)
    y_rec = jnp.dot(rec_ref[...], wr_ref[...], preferred_element_type=F32)
    g_att = jax.nn.sigmoid(ga_ref[...].astype(F32) + gb_ref[0:1, :])
    g_rec = jax.nn.sigmoid(gr_ref[...].astype(F32) + gb_ref[1:2, :])
    o_ref[...] = (g_att * y_att + g_rec * y_rec).astype(o_ref.dtype)
    _run_casts(cast_in, cast_out)


def _merge(att, rec, w_att, w_rec, z, gate_b, casts):
    t = att.shape[0]
    n = w_att.shape[1]
    bm, bn = TILES["merge"]
    grid = (t // bm, n // bn)
    ga_blk, gr_blk = OFF_GA // bn, OFF_GR // bn
    c_in, c_out, c_shapes, c_args = _cast_plan(casts, grid)
    return pl.pallas_call(
        functools.partial(_merge_kernel, n_cast=len(casts)),
        grid=grid,
        in_specs=[pl.BlockSpec((bm, ATTN_W), lambda i, j: (i, 0)),
                  pl.BlockSpec((bm, LRU_W), lambda i, j: (i, 0)),
                  pl.BlockSpec((ATTN_W, bn), lambda i, j: (0, j)),
                  pl.BlockSpec((LRU_W, bn), lambda i, j: (0, j)),
                  pl.BlockSpec((bm, bn), lambda i, j: (i, j + ga_blk)),
                  pl.BlockSpec((bm, bn), lambda i, j: (i, j + gr_blk)),
                  pl.BlockSpec((2, bn), lambda i, j: (0, j))] + c_in,
        out_specs=[pl.BlockSpec((bm, bn), lambda i, j: (i, j))] + c_out,
        out_shape=[jax.ShapeDtypeStruct((t, n), BF16)] + c_shapes,
        compiler_params=_params("parallel", "parallel"),
        name="merge",
    )(att, rec, w_att, w_rec, z, z, gate_b, *c_args)


def _residual_matmul_kernel(*refs, nj, nk, n_cast, emit_xg):
    a_ref, b_ref, x_ref = refs[:3]
    n_in = 3 + emit_xg
    g_ref = refs[3] if emit_xg else None
    cast_in = refs[n_in:n_in + n_cast]
    outs = refs[n_in + n_cast:len(refs) - 1]
    ss_ref = refs[-1]
    o_ref = outs[0]
    xg_ref = outs[1] if emit_xg else None
    r_ref = outs[1 + emit_xg]
    cast_out = outs[2 + emit_xg:]
    j, k = pl.program_id(1), pl.program_id(2)

    def product():
        return jnp.dot(a_ref[...], b_ref[...], preferred_element_type=F32)

    if nk > 1:
        @pl.when(k == 0)
        def _():
            o_ref[...] = x_ref[...] + product()

        @pl.when(jnp.logical_and(k > 0, k < nk - 1))
        def _():
            o_ref[...] += product()

    @pl.when(k == nk - 1)
    def _():
        xn = (o_ref[...] if nk > 1 else x_ref[...]) + product()
        o_ref[...] = xn
        if emit_xg:
            xg_ref[...] = (xn * g_ref[...]).astype(xg_ref.dtype)
        sq = xn * xn
        part = sq[:, 0:LANES]
        for c in range(1, sq.shape[1] // LANES):
            part = part + sq[:, c * LANES:(c + 1) * LANES]

        @pl.when(j == 0)
        def _():
            ss_ref[...] = part

        @pl.when(j > 0)
        def _():
            ss_ref[...] += part

        @pl.when(j == nj - 1)
        def _():
            mean_sq = jnp.sum(ss_ref[...], axis=-1, keepdims=True) * (1.0 / (nj * sq.shape[1]))
            r_ref[...] = jnp.broadcast_to(lax.rsqrt(mean_sq + EPS), r_ref.shape)

    _run_casts(cast_in, cast_out)


def _residual_matmul(a, w, x, g_next, casts, name):
    t, kdim = a.shape
    n = w.shape[1]
    bm, bn, bk = TILES[name]
    nj, nk = n // bn, kdim // bk
    grid = (t // bm, nj, nk)
    emit_xg = g_next is not None
    c_in, c_out, c_shapes, c_args = _cast_plan(casts, grid)
    tile = pl.BlockSpec((bm, bn), lambda i, j, k: (i, j))
    out_specs = [tile] + ([tile] if emit_xg else []) + [pl.BlockSpec((bm, LANES), lambda i, j, k: (i, 0))]
    out_shape = ([jax.ShapeDtypeStruct((t, n), F32)] + ([jax.ShapeDtypeStruct((t, n), BF16)] if emit_xg else [])
                 + [jax.ShapeDtypeStruct((t, LANES), F32)])
    return pl.pallas_call(
        functools.partial(_residual_matmul_kernel, nj=nj, nk=nk, n_cast=len(casts), emit_xg=emit_xg),
        grid=grid,
        in_specs=[pl.BlockSpec((bm, bk), lambda i, j, k: (i, k)),
                  pl.BlockSpec((bk, bn), lambda i, j, k: (k, j)),
                  tile]
                 + ([pl.BlockSpec((1, bn), lambda i, j, k: (0, j))] if emit_xg else []) + c_in,
        out_specs=out_specs + c_out,
        out_shape=out_shape + c_shapes,
        scratch_shapes=[pltpu.VMEM((bm, LANES), F32)],
        compiler_params=_params("parallel", "arbitrary", "arbitrary"),
        name=name,
    )(a, w, x, *([g_next.reshape(1, n)] if emit_xg else []), *c_args)


def _attn_kernel(q_ref, k_ref, v_ref, lq_ref, sg_ref, o_ref, *, lam_init, tq):
    seq = q_ref.shape[1]
    lq = lq_ref[...]
    lam = (jnp.exp(jnp.sum(lq[0:1] * lq[1:2], axis=-1, keepdims=True))
           - jnp.exp(jnp.sum(lq[2:3] * lq[3:4], axis=-1, keepdims=True)) + lam_init)
    contract_last = (((1,), (1,)), ((), ()))
    row = lax.broadcasted_iota(jnp.int32, (tq, tq), 0)
    col = lax.broadcasted_iota(jnp.int32, (tq, tq), 1)
    visible = row >= col
    def scores(i, c):
        off = i * tq
        comp = slice(c * HEAD_DIM, (c + 1) * HEAD_DIM)
        q = q_ref[0, off:off + tq, comp]
        s_d = lax.dot_general(q, k_ref[0, off:off + tq, comp], contract_last, preferred_element_type=F32)
        s_d = jnp.where(visible, s_d, -jnp.inf)
        s_p = None
        if i > 0:
            s_p = lax.dot_general(q, k_ref[0, 0:off, comp], contract_last, preferred_element_type=F32)
        return s_d, s_p

    def numerators(i, c, sc):
        s_d, s_p = sc
        m = jnp.max(s_d, axis=-1, keepdims=True)
        e_p = None
        if i > 0:
            m = jnp.maximum(m, jnp.max(s_p, axis=-1, keepdims=True))
            e_p = jnp.exp2(s_p - m)
        e_d = jnp.exp2(s_d - m)
        denom = jnp.sum(e_d, axis=-1, keepdims=True)
        if i > 0:
            denom = denom + jnp.sum(e_p, axis=-1, keepdims=True)
        return e_d, e_p, (1.0 if c == 0 else lam) / denom

    def weights(i, n0, n1):
        w_d = (n0[0] * n0[2] - n1[0] * n1[2]).astype(BF16)
        w_p = (n0[1] * n0[2] - n1[1] * n1[2]).astype(BF16) if i > 0 else None
        return w_d, w_p

    def finish(i, w):
        off = i * tq
        w_d, w_p = w
        o = jnp.dot(w_d, v_ref[0, off:off + tq, :], preferred_element_type=F32)
        if i > 0:
            o = o + jnp.dot(w_p, v_ref[0, 0:off, :], preferred_element_type=F32)
        o = o * lax.rsqrt(jnp.mean(o * o, axis=-1, keepdims=True) + EPS)
        o = (o * sg_ref[...]) * (1.0 - lam_init)
        o_ref[0, off:off + tq, :] = o.astype(o_ref.dtype)

    n_blocks = seq // tq
    sc = {(0, c): scores(0, c) for c in range(2)}
    w = {}
    for i in range(n_blocks + 1):
        num = []
        for c in range(2):
            if i + 1 < n_blocks:
                sc[i + 1, c] = scores(i + 1, c)
            if i < n_blocks:
                num.append(numerators(i, c, sc.pop((i, c))))
        if i >= 1:
            finish(i - 1, w.pop(i - 1))
        if i < n_blocks:
            w[i] = weights(i, *num)


def _attention(z3, lam_qk, subln_g, lam_init):
    b, s, _ = z3.shape
    tq = TILES["attn_q_rows"]
    kb, vb = OFF_K // HEAD_W, OFF_V // HEAD_W
    return pl.pallas_call(
        functools.partial(_attn_kernel, lam_init=lam_init, tq=tq),
        grid=(b, N_HEADS),
        in_specs=[pl.BlockSpec((1, s, HEAD_W), lambda bi, h: (bi, 0, h)),
                  pl.BlockSpec((1, s, HEAD_W), lambda bi, h: (bi, 0, h + kb)),
                  pl.BlockSpec((1, s, HEAD_W), lambda bi, h: (bi, 0, h + vb)),
                  pl.BlockSpec((4, HEAD_DIM), lambda bi, h: (0, 0)),
                  pl.BlockSpec((1, HEAD_W), lambda bi, h: (0, 0))],
        out_specs=pl.BlockSpec((1, s, HEAD_W), lambda bi, h: (bi, 0, h)),
        out_shape=jax.ShapeDtypeStruct((b, s, ATTN_W), BF16),
        compiler_params=_params("parallel", "parallel"),
        name="diff_attention",
    )(z3, z3, z3, lam_qk, subln_g.reshape(1, HEAD_W))


def _softplus(x):
    return jnp.maximum(x, 0.0) + jnp.log1p(jnp.exp(-jnp.abs(x)))


def _gelu_tanh(x):
    sqrt_2_over_pi = np.sqrt(2 / np.pi).astype(np.float32)
    cdf = 0.5 * (1.0 + jnp.tanh(sqrt_2_over_pi * (x + 0.044715 * (x * x * x))))
    return x * cdf


def _rglru_lanes(x, tail, y, carry, cw, cb, wg, b_r, b_i, lam, row8, row_in_tile):
    ts = x.shape[0]
    n_tiles = ts // SUBLANES
    xr = cb + cw[CONV_W - 1:CONV_W] * x
    for d in range(1, CONV_W):
        rolled = pltpu.roll(x, d, 0)
        head = jnp.where(row8 < d, pltpu.roll(tail, d, 0), rolled[:SUBLANES, :])
        shifted = jnp.concatenate([head, rolled[SUBLANES:, :]], axis=0)
        xr = xr + cw[CONV_W - 1 - d:CONV_W - d] * shifted
    gates = jnp.dot(xr.astype(BF16), wg, preferred_element_type=F32)
    r = jax.nn.sigmoid(gates[:, :LRU_BLOCK_W] + b_r)
    ig = jax.nn.sigmoid(gates[:, LRU_BLOCK_W:] + b_i)
    log_a = -LRU_C * r * _softplus(-lam)
    a = jnp.exp(log_a)
    th = jnp.tanh(log_a)
    bt = jnp.sqrt(-2.0 * th / (1.0 - th)) * (ig * xr)
    a3 = a.reshape(n_tiles, SUBLANES, LRU_BLOCK_W)
    b3 = bt.reshape(n_tiles, SUBLANES, LRU_BLOCK_W)
    d = 1
    while d < SUBLANES:
        keep = row_in_tile >= d
        a_sh = jnp.where(keep, pltpu.roll(a3, d, 1), 1.0)
        b_sh = jnp.where(keep, pltpu.roll(b3, d, 1), 0.0)
        b3 = a3 * b_sh + b3
        a3 = a3 * a_sh
        d *= 2
    tiles = []
    for v in range(n_tiles):
        h_v = a3[v] * carry + b3[v]
        tiles.append(h_v)
        carry = h_v[SUBLANES - 1:SUBLANES, :]
    h = jnp.concatenate(tiles, axis=0)
    return h * _gelu_tanh(y), carry


def _in_proj_kernel(*refs, n_cast, ts, blocks_per_seq):
    xg_ref, r_ref, w_ref, cs_ref, cw_ref, cb_ref, wg_ref, br_ref, bi_ref, lam_ref = refs[:10]
    cast_in = refs[10:10 + n_cast]
    z_ref, rec_ref = refs[10 + n_cast:12 + n_cast]
    cast_out = refs[12 + n_cast:len(refs) - 4]
    lx_s, ly_s, tail_s, carry_s = refs[-4:]
    i, j = pl.program_id(0), pl.program_id(1)
    bm, bn = z_ref.shape
    n_sub = bm // ts
    n_lane_blocks = LRU_W // LRU_BLOCK_W
    chunk = bm // n_lane_blocks
    j_lx, j_ly = OFF_LX // bn, OFF_LY // bn
    j_rg = j_ly + 1

    def project(rows):
        acc = jnp.dot(xg_ref[rows, :], w_ref[...], preferred_element_type=F32)
        return ((acc * r_ref[rows, 0:1]) * cs_ref[...]).astype(z_ref.dtype)

    @pl.when(jnp.logical_and(i == 0, j == 0))
    def _():
        lx_s[bm - BF16_ROWS:bm, :] = jnp.zeros((BF16_ROWS, LRU_W), lx_s.dtype)

    runs_rglru = jnp.logical_and(j >= j_rg, j < j_rg + n_sub)

    @pl.when(jnp.logical_not(runs_rglru))
    def _():
        z = project(slice(None))
        z_ref[...] = z

        @pl.when(j == j_lx)
        def _():
            tail_s[...] = lx_s[bm - BF16_ROWS:bm, :]
            lx_s[...] = z

        @pl.when(j == j_ly)
        def _():
            ly_s[...] = z

    @pl.when(runs_rglru)
    def _():
        sub = j - j_rg
        base = pl.multiple_of(sub * ts, ts)
        before = pl.multiple_of(jnp.maximum(base - BF16_ROWS, 0), BF16_ROWS)
        seq_start = jnp.logical_and(i % blocks_per_seq == 0, sub == 0)
        row8 = lax.broadcasted_iota(jnp.int32, (SUBLANES, LRU_BLOCK_W), 0)
        row_in_tile = lax.broadcasted_iota(jnp.int32, (ts // SUBLANES, SUBLANES, LRU_BLOCK_W), 1)
        for n in range(n_lane_blocks):
            rows = slice(n * chunk, (n + 1) * chunk)
            z_ref[rows, :] = project(rows)
            sl = slice(n * LRU_BLOCK_W, (n + 1) * LRU_BLOCK_W)
            x = lx_s[pl.ds(base, ts), sl].astype(F32)
            prev = jnp.where(sub > 0, lx_s[pl.ds(before, BF16_ROWS), sl], tail_s[:, sl]).astype(F32)
            tail = jnp.where(seq_start, 0.0, prev[BF16_ROWS - SUBLANES:, :])
            carry = jnp.where(seq_start, 0.0, carry_s[0:1, sl])
            y = ly_s[pl.ds(base, ts), sl].astype(F32)
            out, carry = _rglru_lanes(x, tail, y, carry, cw_ref[:, sl], cb_ref[:, sl], wg_ref[n], br_ref[:, sl],
                                      bi_ref[:, sl], lam_ref[:, sl], row8, row_in_tile)
            rec_ref[:, sl] = out.astype(rec_ref.dtype)
            carry_s[0:1, sl] = carry

    _run_casts(cast_in, cast_out)


def _in_proj(xg, r, w, colscale, seq, conv_w, conv_b, w_gates, b_r, b_i, lru_lambda, casts):
    t, k = xg.shape
    n = w.shape[1]
    bm, bn = TILES["in_proj"]
    ts = TILES["rglru_rows"]
    grid = (t // bm, n // bn)
    n_sub = bm // ts
    j_rg = OFF_LY // bn + 1
    assert bn == LRU_W and OFF_LX % bn == 0 and seq % bm == 0 and bm % ts == 0 and j_rg + n_sub <= grid[1]
    assert (bm // (LRU_W // LRU_BLOCK_W)) % BF16_ROWS == 0
    c_in, c_out, c_shapes, c_args = _cast_plan(casts, grid)
    vec = lambda a: a.reshape(1, LRU_W)
    const = lambda i, j: (0, 0)
    return pl.pallas_call(
        functools.partial(_in_proj_kernel, n_cast=len(casts), ts=ts, blocks_per_seq=seq // bm),
        grid=grid,
        in_specs=[pl.BlockSpec((bm, k), lambda i, j: (i, 0)),
                  pl.BlockSpec((bm, LANES), lambda i, j: (i, 0)),
                  pl.BlockSpec((k, bn), lambda i, j: (0, j)),
                  pl.BlockSpec((1, bn), lambda i, j: (0, j)),
                  pl.BlockSpec((CONV_W, LRU_W), const),
                  pl.BlockSpec((1, LRU_W), const),
                  pl.BlockSpec((LRU_W // LRU_BLOCK_W, LRU_BLOCK_W, 2 * LRU_BLOCK_W), lambda i, j: (0, 0, 0)),
                  pl.BlockSpec((1, LRU_W), const),
                  pl.BlockSpec((1, LRU_W), const),
                  pl.BlockSpec((1, LRU_W), const)] + c_in,
        out_specs=[pl.BlockSpec((bm, bn), lambda i, j: (i, j)),
                   pl.BlockSpec((ts, LRU_W), lambda i, j: (i * n_sub + jnp.clip(j - j_rg, 0, n_sub - 1), 0))] + c_out,
        out_shape=[jax.ShapeDtypeStruct((t, n), BF16), jax.ShapeDtypeStruct((t, LRU_W), BF16)] + c_shapes,
        scratch_shapes=[pltpu.VMEM((bm, LRU_W), BF16), pltpu.VMEM((bm, LRU_W), BF16),
                        pltpu.VMEM((BF16_ROWS, LRU_W), BF16), pltpu.VMEM((SUBLANES, LRU_W), F32)],
        compiler_params=_params("arbitrary", "arbitrary"),
        name="in_proj",
    )(xg, r, w, colscale, conv_w, vec(conv_b), w_gates, vec(b_r), vec(b_i), vec(lru_lambda), *c_args)


def kernel(x, norm1_g, w_in, gate_b, lam_qk, subln_g, w_attn_proj, conv_w, conv_b, w_rgate, b_rgate,
           w_igate, b_igate, lru_lambda, w_rec_proj, w_out, norm2_g, w_up, w_down, final_g):
    bsz, seq, d = x.shape
    t = bsz * seq
    x = x.reshape(t, d)
    q_scale = jnp.ones((1, C_IN), F32).at[:, OFF_Q:OFF_Q + ATTN_W].set(HEAD_DIM ** -0.5 * LOG2_E)
    wb_in, wb_att, wb_rec = (w[0].astype(BF16) for w in (w_in, w_attn_proj, w_rec_proj))
    xg, r = _prep(x, norm1_g[0])
    for l in range(DEPTH):
        first, last = l == 0, l == DEPTH - 1
        nxt = lambda *ws: [] if last else [(w, l + 1) for w in ws]
        now = lambda *ws: [(w, 0) for w in ws] if first else []
        lam_init = 0.8 - 0.6 * math.exp(-0.3 * l)
        w_gates = jnp.concatenate([w_rgate[l], w_igate[l]], axis=-1).astype(BF16)
        z, rec, *conv = _in_proj(xg, r, wb_in, q_scale, seq, conv_w[l], conv_b[l], w_gates, b_rgate[l], b_igate[l],
                                 lru_lambda[l], now(w_up) + nxt(w_in))
        if first:
            wb_up = conv.pop(0)
        nb_in = conv
        att = _attention(z.reshape(bsz, seq, C_IN), lam_qk[l], subln_g[l], lam_init)
        m, *conv = _merge(att.reshape(t, ATTN_W), rec, wb_att, wb_rec, z, gate_b[l],
                          now(w_out) + nxt(w_attn_proj, w_rec_proj, w_out))
        if first:
            wb_out = conv.pop(0)
        nb_proj = conv
        x, xg, r = _residual_matmul(m, wb_out, x, norm2_g[l], [], name="out_proj")
        u, *conv = _up_proj(xg, r, wb_up, now(w_down) + nxt(w_up, w_down))
        if first:
            wb_down = conv.pop(0)
        nb_up = conv
        if last:
            x, r = _residual_matmul(u, wb_down, x, None, [], name="down_proj")
        else:
            x, xg, r = _residual_matmul(u, wb_down, x, norm1_g[l + 1], [], name="down_proj")
            (wb_in,), (wb_att, wb_rec, wb_out), (wb_up, wb_down) = nb_in, nb_proj, nb_up
    return _final_norm(x, r, final_g).reshape(bsz, seq, d)
```

```python
import functools
import math

import numpy as np
import jax
import jax.numpy as jnp
from jax import lax
from jax.experimental import pallas as pl
from jax.experimental.pallas import tpu as pltpu

D_MODEL = 4096
DEPTH = 4
N_HEADS = 8
HEAD_DIM = 128
HEAD_W = 2 * HEAD_DIM
ATTN_W = N_HEADS * HEAD_W
LRU_W = 1024
LRU_BLOCKS = 8
LRU_BLOCK_W = LRU_W // LRU_BLOCKS
CONV_W = 4
LRU_C = 8.0
D_FF = 4 * D_MODEL
EPS = 1e-6
C_IN = 3 * ATTN_W + 2 * LRU_W + 2 * D_MODEL
OFF_Q, OFF_K, OFF_V = 0, ATTN_W, 2 * ATTN_W
OFF_LX = 3 * ATTN_W
OFF_LY = OFF_LX + LRU_W
OFF_GA = OFF_LY + LRU_W
OFF_GR = OFF_GA + D_MODEL

LANES = 128
SUBLANES = 8
BF16_ROWS = 16
VMEM_LIMIT = 64 * 1024 * 1024
LOG2_E = math.log2(math.e)

TILES = {
    "norm_rows": 512,
    "in_proj": (1024, 1024),
    "up_proj": (1024, 1024),
    "merge": (1024, 1024),
    "out_proj": (1024, 1024, D_MODEL),
    "down_proj": (1024, 1024, 4096),
    "attn_q_rows": 256,
    "rglru_rows": 256,
}

BF16 = jnp.bfloat16
F32 = jnp.float32


def _params(*sem):
    return pltpu.CompilerParams(dimension_semantics=sem, vmem_limit_bytes=VMEM_LIMIT)


def _row_scale(r_ref):
    return r_ref[:, 0:1]


def _cast_plan(casts, grid):
    steps = math.prod(grid)

    def linear(*g):
        idx = g[0]
        for size, gi in zip(grid[1:], g[1:]):
            idx = idx * size + gi
        return idx

    in_specs, out_specs, out_shapes, args = [], [], [], []
    for w, layer in casts:
        _, k, n = w.shape
        rows = k // steps
        assert rows * steps == k and rows % BF16_ROWS == 0, (w.shape, grid)
        in_specs.append(pl.BlockSpec((None, rows, n), lambda *g, layer=layer: (layer, linear(*g), 0)))
        out_specs.append(pl.BlockSpec((rows, n), lambda *g: (linear(*g), 0)))
        out_shapes.append(jax.ShapeDtypeStruct((k, n), BF16))
        args.append(w)
    return in_specs, out_specs, out_shapes, args


def _run_casts(cast_in, cast_out):
    for src, dst in zip(cast_in, cast_out):
        dst[...] = src[...].astype(dst.dtype)


def _prep_kernel(x_ref, g_ref, xg_ref, r_ref):
    x = x_ref[...]
    r = lax.rsqrt(jnp.mean(x * x, axis=-1, keepdims=True) + EPS)
    xg_ref[...] = (x * g_ref[...]).astype(xg_ref.dtype)
    r_ref[...] = jnp.broadcast_to(r, r_ref.shape)


def _prep(x, g):
    t, d = x.shape
    bm = TILES["norm_rows"]
    return pl.pallas_call(
        _prep_kernel,
        grid=(t // bm,),
        in_specs=[pl.BlockSpec((bm, d), lambda i: (i, 0)),
                  pl.BlockSpec((1, d), lambda i: (0, 0))],
        out_specs=[pl.BlockSpec((bm, d), lambda i: (i, 0)),
                   pl.BlockSpec((bm, LANES), lambda i: (i, 0))],
        out_shape=[jax.ShapeDtypeStruct((t, d), BF16),
                   jax.ShapeDtypeStruct((t, LANES), F32)],
        compiler_params=_params("parallel"),
        name="norm_prep",
    )(x, g.reshape(1, d))


def _final_norm_kernel(x_ref, r_ref, g_ref, o_ref):
    o_ref[...] = (x_ref[...] * _row_scale(r_ref)) * g_ref[...]


def _final_norm(x, r, g):
    t, d = x.shape
    bm = TILES["norm_rows"]
    return pl.pallas_call(
        _final_norm_kernel,
        grid=(t // bm,),
        in_specs=[pl.BlockSpec((bm, d), lambda i: (i, 0)),
                  pl.BlockSpec((bm, LANES), lambda i: (i, 0)),
                  pl.BlockSpec((1, d), lambda i: (0, 0))],
        out_specs=pl.BlockSpec((bm, d), lambda i: (i, 0)),
        out_shape=jax.ShapeDtypeStruct((t, d), F32),
        compiler_params=_params("parallel"),
        name="final_norm",
    )(x, r, g.reshape(1, d))


def _up_proj_kernel(*refs, n_cast):
    a_ref, r_ref, b_ref = refs[:3]
    cast_in = refs[3:3 + n_cast]
    o_ref = refs[3 + n_cast]
    cast_out = refs[4 + n_cast:]
    acc = jnp.dot(a_ref[...], b_ref[...], preferred_element_type=F32)
    o_ref[...] = jnp.square(jnp.maximum(acc * _row_scale(r_ref), 0.0)).astype(o_ref.dtype)
    _run_casts(cast_in, cast_out)


def _up_proj(xg, r, w, casts):
    t, k = xg.shape
    n = w.shape[1]
    bm, bn = TILES["up_proj"]
    grid = (t // bm, n // bn)
    c_in, c_out, c_shapes, c_args = _cast_plan(casts, grid)
    return pl.pallas_call(
        functools.partial(_up_proj_kernel, n_cast=len(casts)),
        grid=grid,
        in_specs=[pl.BlockSpec((bm, k), lambda i, j: (i, 0)),
                  pl.BlockSpec((bm, LANES), lambda i, j: (i, 0)),
                  pl.BlockSpec((k, bn), lambda i, j: (0, j))] + c_in,
        out_specs=[pl.BlockSpec((bm, bn), lambda i, j: (i, j))] + c_out,
        out_shape=[jax.ShapeDtypeStruct((t, n), BF16)] + c_shapes,
        compiler_params=_params("parallel", "parallel"),
        name="up_proj",
    )(xg, r, w, *c_args)


def _merge_kernel(*refs, n_cast):
    att_ref, rec_ref, wa_ref, wr_ref, ga_ref, gr_ref, gb_ref = refs[:7]
    cast_in = refs[7:7 + n_cast]
    o_ref = refs[7 + n_cast]
    cast_out = refs[8 + n_cast:]
    y_att = jnp.dot(att_ref[...], wa_ref[...], preferred_element_type=F32)
    y_rec = jnp.dot(rec_ref[...], wr_ref[...], preferred_element_type=F32)
    g_att = jax.nn.sigmoid(ga_ref[...].astype(F32) + gb_ref[0:1, :])
    g_rec = jax.nn.sigmoid(gr_ref[...].astype(F32) + gb_ref[1:2, :])
    o_ref[...] = (g_att * y_att + g_rec * y_rec).astype(o_ref.dtype)
    _run_casts(cast_in, cast_out)


def _merge(att, rec, w_att, w_rec, z, gate_b, casts):
    t = att.shape[0]
    n = w_att.shape[1]
    bm, bn = TILES["merge"]
    grid = (t // bm, n // bn)
    ga_blk, gr_blk = OFF_GA // bn, OFF_GR // bn
    c_in, c_out, c_shapes, c_args = _cast_plan(casts, grid)
    return pl.pallas_call(
        functools.partial(_merge_kernel, n_cast=len(casts)),
        grid=grid,
        in_specs=[pl.BlockSpec((bm, ATTN_W), lambda i, j: (i, 0)),
                  pl.BlockSpec((bm, LRU_W), lambda i, j: (i, 0)),
                  pl.BlockSpec((ATTN_W, bn), lambda i, j: (0, j)),
                  pl.BlockSpec((LRU_W, bn), lambda i, j: (0, j)),
                  pl.BlockSpec((bm, bn), lambda i, j: (i, j + ga_blk)),
                  pl.BlockSpec((bm, bn), lambda i, j: (i, j + gr_blk)),
                  pl.BlockSpec((2, bn), lambda i, j: (0, j))] + c_in,
        out_specs=[pl.BlockSpec((bm, bn), lambda i, j: (i, j))] + c_out,
        out_shape=[jax.ShapeDtypeStruct((t, n), BF16)] + c_shapes,
        compiler_params=_params("parallel", "parallel"),
        name="merge",
    )(att, rec, w_att, w_rec, z, z, gate_b, *c_args)


def _residual_matmul_kernel(*refs, nj, nk, n_cast, emit_xg):
    a_ref, b_ref, x_ref = refs[:3]
    n_in = 3 + emit_xg
    g_ref = refs[3] if emit_xg else None
    cast_in = refs[n_in:n_in + n_cast]
    outs = refs[n_in + n_cast:len(refs) - 1]
    ss_ref = refs[-1]
    o_ref = outs[0]
    xg_ref = outs[1] if emit_xg else None
    r_ref = outs[1 + emit_xg]
    cast_out = outs[2 + emit_xg:]
    j, k = pl.program_id(1), pl.program_id(2)

    def product():
        return jnp.dot(a_ref[...], b_ref[...], preferred_element_type=F32)

    if nk > 1:
        @pl.when(k == 0)
        def _():
            o_ref[...] = x_ref[...] + product()

        @pl.when(jnp.logical_and(k > 0, k < nk - 1))
        def _():
            o_ref[...] += product()

    @pl.when(k == nk - 1)
    def _():
        xn = (o_ref[...] if nk > 1 else x_ref[...]) + product()
        o_ref[...] = xn
        if emit_xg:
            xg_ref[...] = (xn * g_ref[...]).astype(xg_ref.dtype)
        sq = xn * xn
        part = sq[:, 0:LANES]
        for c in range(1, sq.shape[1] // LANES):
            part = part + sq[:, c * LANES:(c + 1) * LANES]

        @pl.when(j == 0)
        def _():
            ss_ref[...] = part

        @pl.when(j > 0)
        def _():
            ss_ref[...] += part

        @pl.when(j == nj - 1)
        def _():
            mean_sq = jnp.sum(ss_ref[...], axis=-1, keepdims=True) * (1.0 / (nj * sq.shape[1]))
            r_ref[...] = jnp.broadcast_to(lax.rsqrt(mean_sq + EPS), r_ref.shape)

    _run_casts(cast_in, cast_out)


def _residual_matmul(a, w, x, g_next, casts, name):
    t, kdim = a.shape
    n = w.shape[1]
    bm, bn, bk = TILES[name]
    nj, nk = n // bn, kdim // bk
    grid = (t // bm, nj, nk)
    emit_xg = g_next is not None
    c_in, c_out, c_shapes, c_args = _cast_plan(casts, grid)
    tile = pl.BlockSpec((bm, bn), lambda i, j, k: (i, j))
    out_specs = [tile] + ([tile] if emit_xg else []) + [pl.BlockSpec((bm, LANES), lambda i, j, k: (i, 0))]
    out_shape = ([jax.ShapeDtypeStruct((t, n), F32)] + ([jax.ShapeDtypeStruct((t, n), BF16)] if emit_xg else [])
                 + [jax.ShapeDtypeStruct((t, LANES), F32)])
    return pl.pallas_call(
        functools.partial(_residual_matmul_kernel, nj=nj, nk=nk, n_cast=len(casts), emit_xg=emit_xg),
        grid=grid,
        in_specs=[pl.BlockSpec((bm, bk), lambda i, j, k: (i, k)),
                  pl.BlockSpec((bk, bn), lambda i, j, k: (k, j)),
                  tile]
                 + ([pl.BlockSpec((1, bn), lambda i, j, k: (0, j))] if emit_xg else []) + c_in,
        out_specs=out_specs + c_out,
        out_shape=out_shape + c_shapes,
        scratch_shapes=[pltpu.VMEM((bm, LANES), F32)],
        compiler_params=_params("parallel", "arbitrary", "arbitrary"),
        name=name,
    )(a, w, x, *([g_next.reshape(1, n)] if emit_xg else []), *c_args)


def _attn_kernel(q_ref, k_ref, v_ref, lq_ref, sg_ref, o_ref, *, lam_init, tq):
    seq = q_ref.shape[1]
    lq = lq_ref[...]
    lam = (jnp.exp(jnp.sum(lq[0:1] * lq[1:2], axis=-1, keepdims=True))
           - jnp.exp(jnp.sum(lq[2:3] * lq[3:4], axis=-1, keepdims=True)) + lam_init)
    contract_last = (((1,), (1,)), ((), ()))
    row = lax.broadcasted_iota(jnp.int32, (tq, tq), 0)
    col = lax.broadcasted_iota(jnp.int32, (tq, tq), 1)
    visible = row >= col
    def scores(i, c):
        off = i * tq
        comp = slice(c * HEAD_DIM, (c + 1) * HEAD_DIM)
        q = q_ref[0, off:off + tq, comp]
        s_d = lax.dot_general(q, k_ref[0, off:off + tq, comp], contract_last, preferred_element_type=F32)
        s_d = jnp.where(visible, s_d, -jnp.inf)
        s_p = None
        if i > 0:
            s_p = lax.dot_general(q, k_ref[0, 0:off, comp], contract_last, preferred_element_type=F32)
        return s_d, s_p

    def numerators(i, c, sc):
        s_d, s_p = sc
        m = jnp.max(s_d, axis=-1, keepdims=True)
        e_p = None
        if i > 0:
            m = jnp.maximum(m, jnp.max(s_p, axis=-1, keepdims=True))
            e_p = jnp.exp2(s_p - m)
        e_d = jnp.exp2(s_d - m)
        denom = jnp.sum(e_d, axis=-1, keepdims=True)
        if i > 0:
            denom = denom + jnp.sum(e_p, axis=-1, keepdims=True)
        return e_d, e_p, (1.0 if c == 0 else lam) / denom

    def weights(i, n0, n1):
        w_d = (n0[0] * n0[2] - n1[0] * n1[2]).astype(BF16)
        w_p = (n0[1] * n0[2] - n1[1] * n1[2]).astype(BF16) if i > 0 else None
        return w_d, w_p

    def finish(i, w):
        off = i * tq
        w_d, w_p = w
        o = jnp.dot(w_d, v_ref[0, off:off + tq, :], preferred_element_type=F32)
        if i > 0:
            o = o + jnp.dot(w_p, v_ref[0, 0:off, :], preferred_element_type=F32)
        o = o * lax.rsqrt(jnp.mean(o * o, axis=-1, keepdims=True) + EPS)
        o = (o * sg_ref[...]) * (1.0 - lam_init)
        o_ref[0, off:off + tq, :] = o.astype(o_ref.dtype)

    n_blocks = seq // tq
    sc = {(0, c): scores(0, c) for c in range(2)}
    w = {}
    for i in range(n_blocks + 1):
        num = []
        for c in range(2):
            if i + 1 < n_blocks:
                sc[i + 1, c] = scores(i + 1, c)
            if i < n_blocks:
                num.append(numerators(i, c, sc.pop((i, c))))
        if i >= 1:
            finish(i - 1, w.pop(i - 1))
        if i < n_blocks:
            w[i] = weights(i, *num)


def _attention(z3, lam_qk, subln_g, lam_init):
    b, s, _ = z3.shape
    tq = TILES["attn_q_rows"]
    kb, vb = OFF_K // HEAD_W, OFF_V // HEAD_W
    return pl.pallas_call(
        functools.partial(_attn_kernel, lam_init=lam_init, tq=tq),
        grid=(b, N_HEADS),
        in_specs=[pl.BlockSpec((1, s, HEAD_W), lambda bi, h: (bi, 0, h)),
                  pl.BlockSpec((1, s, HEAD_W), lambda bi, h: (bi, 0, h + kb)),
                  pl.BlockSpec((1, s, HEAD_W), lambda bi, h: (bi, 0, h + vb)),
                  pl.BlockSpec((4, HEAD_DIM), lambda bi, h: (0, 0)),
                  pl.BlockSpec((1, HEAD_W), lambda bi, h: (0, 0))],
        out_specs=pl.BlockSpec((1, s, HEAD_W), lambda bi, h: (bi, 0, h)),
        out_shape=jax.ShapeDtypeStruct((b, s, ATTN_W), BF16),
        compiler_params=_params("parallel", "parallel"),
        name="diff_attention",
    )(z3, z3, z3, lam_qk, subln_g.reshape(1, HEAD_W))


def _softplus(x):
    return jnp.maximum(x, 0.0) + jnp.log1p(jnp.exp(-jnp.abs(x)))


def _gelu_tanh(x):
    sqrt_2_over_pi = np.sqrt(2 / np.pi).astype(np.float32)
    cdf = 0.5 * (1.0 + jnp.tanh(sqrt_2_over_pi * (x + 0.044715 * (x * x * x))))
    return x * cdf


def _rglru_lanes(x, tail, y, carry, cw, cb, wg, b_r, b_i, lam, row8, row_in_tile):
    ts = x.shape[0]
    n_tiles = ts // SUBLANES
    xr = cb + cw[CONV_W - 1:CONV_W] * x
    for d in range(1, CONV_W):
        rolled = pltpu.roll(x, d, 0)
        head = jnp.where(row8 < d, pltpu.roll(tail, d, 0), rolled[:SUBLANES, :])
        shifted = jnp.concatenate([head, rolled[SUBLANES:, :]], axis=0)
        xr = xr + cw[CONV_W - 1 - d:CONV_W - d] * shifted
    gates = jnp.dot(xr.astype(BF16), wg, preferred_element_type=F32)
    r = jax.nn.sigmoid(gates[:, :LRU_BLOCK_W] + b_r)
    ig = jax.nn.sigmoid(gates[:, LRU_BLOCK_W:] + b_i)
    log_a = -LRU_C * r * _softplus(-lam)
    a = jnp.exp(log_a)
    th = jnp.tanh(log_a)
    bt = jnp.sqrt(-2.0 * th / (1.0 - th)) * (ig * xr)
    a3 = a.reshape(n_tiles, SUBLANES, LRU_BLOCK_W)
    b3 = bt.reshape(n_tiles, SUBLANES, LRU_BLOCK_W)
    d = 1
    while d < SUBLANES:
        keep = row_in_tile >= d
        a_sh = jnp.where(keep, pltpu.roll(a3, d, 1), 1.0)
        b_sh = jnp.where(keep, pltpu.roll(b3, d, 1), 0.0)
        b3 = a3 * b_sh + b3
        a3 = a3 * a_sh
        d *= 2
    tiles = []
    for v in range(n_tiles):
        h_v = a3[v] * carry + b3[v]
        tiles.append(h_v)
        carry = h_v[SUBLANES - 1:SUBLANES, :]
    h = jnp.concatenate(tiles, axis=0)
    return h * _gelu_tanh(y), carry


def _in_proj_kernel(*refs, n_cast, ts, blocks_per_seq):
    xg_ref, r_ref, w_ref, cs_ref, cw_ref, cb_ref, wg_ref, br_ref, bi_ref, lam_ref = refs[:10]
    cast_in = refs[10:10 + n_cast]
    z_ref, rec_ref = refs[10 + n_cast:12 + n_cast]
    cast_out = refs[12 + n_cast:len(refs) - 4]
    lx_s, ly_s, tail_s, carry_s = refs[-4:]
    i, j = pl.program_id(0), pl.program_id(1)
    bm, bn = z_ref.shape
    n_sub = bm // ts
    n_lane_blocks = LRU_W // LRU_BLOCK_W
    chunk = bm // n_lane_blocks
    j_lx, j_ly = OFF_LX // bn, OFF_LY // bn
    j_rg = j_ly + 1

    def project(rows):
        acc = jnp.dot(xg_ref[rows, :], w_ref[...], preferred_element_type=F32)
        return ((acc * r_ref[rows, 0:1]) * cs_ref[...]).astype(z_ref.dtype)

    @pl.when(jnp.logical_and(i == 0, j == 0))
    def _():
        lx_s[bm - BF16_ROWS:bm, :] = jnp.zeros((BF16_ROWS, LRU_W), lx_s.dtype)

    runs_rglru = jnp.logical_and(j >= j_rg, j < j_rg + n_sub)

    @pl.when(jnp.logical_not(runs_rglru))
    def _():
        z = project(slice(None))
        z_ref[...] = z

        @pl.when(j == j_lx)
        def _():
            tail_s[...] = lx_s[bm - BF16_ROWS:bm, :]
            lx_s[...] = z

        @pl.when(j == j_ly)
        def _():
            ly_s[...] = z

    @pl.when(runs_rglru)
    def _():
        sub = j - j_rg
        base = pl.multiple_of(sub * ts, ts)
        before = pl.multiple_of(jnp.maximum(base - BF16_ROWS, 0), BF16_ROWS)
        seq_start = jnp.logical_and(i % blocks_per_seq == 0, sub == 0)
        row8 = lax.broadcasted_iota(jnp.int32, (SUBLANES, LRU_BLOCK_W), 0)
        row_in_tile = lax.broadcasted_iota(jnp.int32, (ts // SUBLANES, SUBLANES, LRU_BLOCK_W), 1)
        for n in range(n_lane_blocks):
            rows = slice(n * chunk, (n + 1) * chunk)
            z_ref[rows, :] = project(rows)
            sl = slice(n * LRU_BLOCK_W, (n + 1) * LRU_BLOCK_W)
            x = lx_s[pl.ds(base, ts), sl].astype(F32)
            prev = jnp.where(sub > 0, lx_s[pl.ds(before, BF16_ROWS), sl], tail_s[:, sl]).astype(F32)
            tail = jnp.where(seq_start, 0.0, prev[BF16_ROWS - SUBLANES:, :])
            carry = jnp.where(seq_start, 0.0, carry_s[0:1, sl])
            y = ly_s[pl.ds(base, ts), sl].astype(F32)
            out, carry = _rglru_lanes(x, tail, y, carry, cw_ref[:, sl], cb_ref[:, sl], wg_ref[n], br_ref[:, sl],
                                      bi_ref[:, sl], lam_ref[:, sl], row8, row_in_tile)
            rec_ref[:, sl] = out.astype(rec_ref.dtype)
            carry_s[0:1, sl] = carry

    _run_casts(cast_in, cast_out)


def _in_proj(xg, r, w, colscale, seq, conv_w, conv_b, w_gates, b_r, b_i, lru_lambda, casts):
    t, k = xg.shape
    n = w.shape[1]
    bm, bn = TILES["in_proj"]
    ts = TILES["rglru_rows"]
    grid = (t // bm, n // bn)
    n_sub = bm // ts
    j_rg = OFF_LY // bn + 1
    assert bn == LRU_W and OFF_LX % bn == 0 and seq % bm == 0 and bm % ts == 0 and j_rg + n_sub <= grid[1]
    assert (bm // (LRU_W // LRU_BLOCK_W)) % BF16_ROWS == 0
    c_in, c_out, c_shapes, c_args = _cast_plan(casts, grid)
    vec = lambda a: a.reshape(1, LRU_W)
    const = lambda i, j: (0, 0)
    return pl.pallas_call(
        functools.partial(_in_proj_kernel, n_cast=len(casts), ts=ts, blocks_per_seq=seq // bm),
        grid=grid,
        in_specs=[pl.BlockSpec((bm, k), lambda i, j: (i, 0)),
                  pl.BlockSpec((bm, LANES), lambda i, j: (i, 0)),
                  pl.BlockSpec((k, bn), lambda i, j: (0, j)),
                  pl.BlockSpec((1, bn), lambda i, j: (0, j)),
                  pl.BlockSpec((CONV_W, LRU_W), const),
                  pl.BlockSpec((1, LRU_W), const),
                  pl.BlockSpec((LRU_W // LRU_BLOCK_W, LRU_BLOCK_W, 2 * LRU_BLOCK_W), lambda i, j: (0, 0, 0)),
                  pl.BlockSpec((1, LRU_W), const),
                  pl.BlockSpec((1, LRU_W), const),
                  pl.BlockSpec((1, LRU_W), const)] + c_in,
        out_specs=[pl.BlockSpec((bm, bn), lambda i, j: (i, j)),
                   pl.BlockSpec((ts, LRU_W), lambda i, j: (i * n_sub + jnp.clip(j - j_rg, 0, n_sub - 1), 0))] + c_out,
        out_shape=[jax.ShapeDtypeStruct((t, n), BF16), jax.ShapeDtypeStruct((t, LRU_W), BF16)] + c_shapes,
        scratch_shapes=[pltpu.VMEM((bm, LRU_W), BF16), pltpu.VMEM((bm, LRU_W), BF16),
                        pltpu.VMEM((BF16_ROWS, LRU_W), BF16), pltpu.VMEM((SUBLANES, LRU_W), F32)],
        compiler_params=_params("arbitrary", "arbitrary"),
        name="in_proj",
    )(xg, r, w, colscale, conv_w, vec(conv_b), w_gates, vec(b_r), vec(b_i), vec(lru_lambda), *c_args)


def kernel(x, norm1_g, w_in, gate_b, lam_qk, subln_g, w_attn_proj, conv_w, conv_b, w_rgate, b_rgate,
           w_igate, b_igate, lru_lambda, w_rec_proj, w_out, norm2_g, w_up, w_down, final_g):
    bsz, seq, d = x.shape
    t = bsz * seq
    x = x.reshape(t, d)
    q_scale = jnp.ones((1, C_IN), F32).at[:, OFF_Q:OFF_Q + ATTN_W].set(HEAD_DIM ** -0.5 * LOG2_E)
    wb_in, wb_att, wb_rec = (w[0].astype(BF16) for w in (w_in, w_attn_proj, w_rec_proj))
    xg, r = _prep(x, norm1_g[0])
    for l in range(DEPTH):
        first, last = l == 0, l == DEPTH - 1
        nxt = lambda *ws: [] if last else [(w, l + 1) for w in ws]
        now = lambda *ws: [(w, 0) for w in ws] if first else []
        lam_init = 0.8 - 0.6 * math.exp(-0.3 * l)
        w_gates = jnp.concatenate([w_rgate[l], w_igate[l]], axis=-1).astype(BF16)
        z, rec, *conv = _in_proj(xg, r, wb_in, q_scale, seq, conv_w[l], conv_b[l], w_gates, b_rgate[l], b_igate[l],
                                 lru_lambda[l], now(w_up) + nxt(w_in))
        if first:
            wb_up = conv.pop(0)
        nb_in = conv
        att = _attention(z.reshape(bsz, seq, C_IN), lam_qk[l], subln_g[l], lam_init)
        m, *conv = _merge(att.reshape(t, ATTN_W), rec, wb_att, wb_rec, z, gate_b[l],
                          now(w_out) + nxt(w_attn_proj, w_rec_proj, w_out))
        if first:
            wb_out = conv.pop(0)
        nb_proj = conv
        x, xg, r = _residual_matmul(m, wb_out, x, norm2_g[l], [], name="out_proj")
        u, *conv = _up_proj(xg, r, wb_up, now(w_down) + nxt(w_up, w_down))
        if first:
            wb_down = conv.pop(0)
        nb_up = conv
        if last:
            x, r = _residual_matmul(u, wb_down, x, None, [], name="down_proj")
        else:
            x, xg, r = _residual_matmul(u, wb_down, x, norm1_g[l + 1], [], name="down_proj")
            (wb_in,), (wb_att, wb_rec, wb_out), (wb_up, wb_down) = nb_in, nb_proj, nb_up
    return _final_norm(x, r, final_g).reshape(bsz, seq, d)
```
